```python
import math
import jax, jax.numpy as jnp
from jax import lax
import numpy as np

D_MODEL = 1024
BATCH = 16
SEQ = 2048
DEPTH = 2

CHUNK = 64
D_MIX = D_MODEL
N_MIXERS = 4
GROUP_W = D_MIX // N_MIXERS
POOL_CH = GROUP_W
POOL_WINDOWS = (2, 4, 8, 16)
POOL_GROUPS = len(POOL_WINDOWS)
POOL_GROUP_CH = POOL_CH // POOL_GROUPS
SGU_CH = GROUP_W
SGU_HEADS = 4
SGU_HEAD_CH = SGU_CH // SGU_HEADS
SGU_BLOCK = 128
SCONV_CH = GROUP_W
SCONV_WIDTH = 3
SSD_INNER = GROUP_W
SSD_HEADDIM = 64
SSD_HEADS = SSD_INNER // SSD_HEADDIM
SSD_GROUPS = 2
SSD_STATE = 128
SSD_CONV = 4
SSD_CHUNK = CHUNK
SSD_BC = SSD_GROUPS * SSD_STATE
SSD_CONV_CH = SSD_INNER + 2 * SSD_BC
IN_COLS = POOL_CH + 2 * SGU_CH + 3 * SCONV_CH + 2 * SSD_INNER + 2 * SSD_BC + SSD_HEADS
D_FF = ((8 * D_MODEL + 3 * 256 - 1) // (3 * 256)) * 256
EPS = 1e-6

kernel_name = 'hybrid_parallel_head_group_streaming_encoder'


def _rmsnorm(x, g):
    xf = x.astype(jnp.float32)
    y = xf * lax.rsqrt(jnp.mean(jnp.square(xf), axis=-1, keepdims=True) + EPS)
    return (y * g.astype(jnp.float32)).astype(x.dtype)


def _split_columns(proj):
    sizes = [POOL_CH, SGU_CH, SGU_CH, SCONV_CH, SCONV_CH, SCONV_CH,
             SSD_INNER, SSD_INNER, SSD_BC, SSD_BC, SSD_HEADS]
    idx = [int(v) for v in np.cumsum(sizes)[:-1]]
    return jnp.split(proj, idx, axis=-1)


def _causal_dwconv(x, w):
    k_taps = w.shape[0]
    s = x.shape[1]
    xp = jnp.pad(x, ((0, 0), (k_taps - 1, 0), (0, 0)))
    out = xp[:, 0:s] * w[0]
    for k in range(1, k_taps):
        out = out + xp[:, k:k + s] * w[k]
    return out


def _pool_mixer(xp, w, bias, scale):
    b, s, _ = xp.shape
    xg = xp.reshape(b, s, POOL_GROUPS, POOL_GROUP_CH)
    xf = xg.astype(jnp.float32)
    cs = jnp.cumsum(xf, axis=1)
    pos = jnp.arange(s, dtype=jnp.float32)
    means = []
    for g, win in enumerate(POOL_WINDOWS):
        c = cs[:, :, g]
        lagged = jnp.pad(c, ((0, 0), (win, 0), (0, 0)))[:, :s]
        count = jnp.minimum(pos + 1.0, float(win))[:, None]
        means.append((c - lagged) / count)
    pooled = (jnp.stack(means, axis=2) - xf).astype(xp.dtype)
    y = jnp.einsum('bsgc,gcd->bsgd', pooled, w).reshape(b, s, POOL_CH) + bias
    return y * scale


def _spatial_gating(u, v, ln_g, ln_b, w_s, b_s):
    b, s, _ = v.shape
    vf = v.astype(jnp.float32)
    mu = jnp.mean(vf, axis=-1, keepdims=True)
    var = jnp.mean(jnp.square(vf - mu), axis=-1, keepdims=True)
    vn = ((vf - mu) * lax.rsqrt(var + EPS)).astype(v.dtype) * ln_g + ln_b
    nb = s // SGU_BLOCK
    vb = vn.reshape(b, nb, SGU_BLOCK, SGU_HEADS, SGU_HEAD_CH)
    chunk_id = jnp.arange(SGU_BLOCK) // CHUNK
    mask = chunk_id[:, None] >= chunk_id[None, :]
    w = jnp.where(mask[None], w_s, jnp.zeros_like(w_s))
    mixed = jnp.einsum('hij,bnjhc->bnihc', w, vb) + b_s.T[None, None, :, :, None]
    return u * mixed.reshape(b, s, SGU_CH)


def _short_conv_mixer(bg, cg, h, w):
    return bg * _causal_dwconv(cg * h, w)


def _ssd_mixer(z, xs, bm, cm, dt, conv_w, conv_b, dt_bias, a_log, d_skip, norm_g):
    dtype = z.dtype
    f32 = jnp.float32
    b, s, _ = xs.shape
    nc = s // SSD_CHUNK
    xbc = jax.nn.silu(_causal_dwconv(jnp.concatenate([xs, bm, cm], axis=-1), conv_w) + conv_b).astype(f32)
    xh = xbc[..., :SSD_INNER].reshape(b, nc, SSD_CHUNK, SSD_HEADS, SSD_HEADDIM)
    rep = SSD_HEADS // SSD_GROUPS
    bh = jnp.repeat(xbc[..., SSD_INNER:SSD_INNER + SSD_BC].reshape(b, nc, SSD_CHUNK, SSD_GROUPS, SSD_STATE), rep, axis=3)
    ch = jnp.repeat(xbc[..., SSD_INNER + SSD_BC:].reshape(b, nc, SSD_CHUNK, SSD_GROUPS, SSD_STATE), rep, axis=3)
    delta = jax.nn.softplus(dt.astype(f32) + dt_bias.astype(f32))
    a = -jnp.exp(a_log.astype(f32))
    delta_c = delta.reshape(b, nc, SSD_CHUNK, SSD_HEADS)
    a_cs = jnp.cumsum((delta_c * a).transpose(0, 3, 1, 2), axis=-1)
    x_dt = xh * delta_c[..., None]
    causal = jnp.tril(jnp.ones((SSD_CHUNK, SSD_CHUNK), dtype=bool))
    seg = a_cs[..., :, None] - a_cs[..., None, :]
    decay = jnp.exp(jnp.where(causal, seg, -jnp.inf))
    scores = jnp.einsum('bclhn,bcshn->bhcls', ch, bh) * decay
    y_diag = jnp.einsum('bhcls,bcshp->bclhp', scores, x_dt)
    to_end = jnp.exp(a_cs[..., -1:] - a_cs)
    states = jnp.einsum('bclhn,bhcl,bclhp->bchpn', bh, to_end, x_dt)
    chunk_decay = jnp.exp(a_cs[..., -1])

    def step(carry, inp):
        st, dec = inp
        return carry * dec[..., None, None] + st, carry

    init = jnp.zeros((b, SSD_HEADS, SSD_HEADDIM, SSD_STATE), f32)
    _, prev = lax.scan(step, init, (states.transpose(1, 0, 2, 3, 4), chunk_decay.transpose(2, 0, 1)))
    prev = prev.transpose(1, 0, 2, 3, 4)
    y_off = jnp.einsum('bclhn,bchpn,bhcl->bclhp', ch, prev, jnp.exp(a_cs))
    y = y_diag + y_off + xh * d_skip.astype(f32)[:, None]
    y = y.reshape(b, s, SSD_INNER) * jax.nn.silu(z.astype(f32))
    yg = y.reshape(b, s, SSD_GROUPS, SSD_INNER // SSD_GROUPS)
    yg = yg * lax.rsqrt(jnp.mean(jnp.square(yg), axis=-1, keepdims=True) + EPS)
    return (yg.reshape(b, s, SSD_INNER) * norm_g.astype(f32)).astype(dtype)


def setup_inputs(seed: int = 0) -> dict:
    key = jax.random.key(seed)
    ks = jax.random.split(key, 24)

    def nrm(k, shape, scale):
        return jax.random.normal(k, shape, jnp.float32) * scale

    L = DEPTH
    u = jax.random.uniform(ks[14], (L, SSD_HEADS), jnp.float32)
    dt0 = jnp.exp(u * (math.log(0.1) - math.log(1e-3)) + math.log(1e-3))
    dt_bias = dt0 + jnp.log(-jnp.expm1(-dt0))
    a_init = jax.random.uniform(ks[15], (L, SSD_HEADS), jnp.float32, 1.0, 16.0)
    return {
        'x': nrm(ks[0], (BATCH, SEQ, D_MODEL), 1.0),
        'norm_mix_g': 1.0 + nrm(ks[1], (L, D_MODEL), 0.02),
        'w_in': nrm(ks[2], (L, D_MODEL, IN_COLS), D_MODEL ** -0.5),
        'pool_w': nrm(ks[3], (L, POOL_GROUPS, POOL_GROUP_CH, POOL_GROUP_CH), POOL_GROUP_CH ** -0.5),
        'pool_b': nrm(ks[4], (L, POOL_CH), 0.02),
        'pool_scale': 1.0 + nrm(ks[5], (L, POOL_CH), 0.1),
        'sgu_ln_g': 1.0 + nrm(ks[6], (L, SGU_CH), 0.02),
        'sgu_ln_b': nrm(ks[7], (L, SGU_CH), 0.02),
        'sgu_w': nrm(ks[8], (L, SGU_HEADS, SGU_BLOCK, SGU_BLOCK), SGU_BLOCK ** -0.5),
        'sgu_b': 1.0 + nrm(ks[9], (L, SGU_HEADS, SGU_BLOCK), 0.1),
        'sconv_w': nrm(ks[10], (L, SCONV_WIDTH, SCONV_CH), SCONV_WIDTH ** -0.5),
        'ssd_conv_w': nrm(ks[11], (L, SSD_CONV, SSD_CONV_CH), SSD_CONV ** -0.5),
        'ssd_conv_b': nrm(ks[12], (L, SSD_CONV_CH), 0.02),
        'ssd_dt_bias': dt_bias,
        'ssd_a_log': jnp.log(a_init),
        'ssd_d': 1.0 + nrm(ks[13], (L, SSD_HEADS), 0.1),
        'ssd_norm_g': 1.0 + nrm(ks[16], (L, SSD_INNER), 0.02),
        'w_out': nrm(ks[17], (L, D_MIX, D_MODEL), D_MIX ** -0.5),
        'norm_ffn_g': 1.0 + nrm(ks[18], (L, D_MODEL), 0.02),
        'w_gate': nrm(ks[19], (L, D_MODEL, D_FF), D_MODEL ** -0.5),
        'w_up': nrm(ks[20], (L, D_MODEL, D_FF), D_MODEL ** -0.5),
        'w_down': nrm(ks[21], (L, D_FF, D_MODEL), D_FF ** -0.5),
        'final_norm_g': 1.0 + nrm(ks[22], (D_MODEL,), 0.02),
    }


def reference(x, norm_mix_g, w_in, pool_w, pool_b, pool_scale, sgu_ln_g, sgu_ln_b, sgu_w, sgu_b,
              sconv_w, ssd_conv_w, ssd_conv_b, ssd_dt_bias, ssd_a_log, ssd_d, ssd_norm_g,
              w_out, norm_ffn_g, w_gate, w_up, w_down, final_norm_g):
    h = x
    for l in range(DEPTH):
        hn = _rmsnorm(h, norm_mix_g[l])
        proj = jnp.einsum('bsd,de->bse', hn, w_in[l])
        (p_pool, p_u, p_v, p_cb, p_cc, p_ch,
         p_z, p_x, p_b, p_c, p_dt) = _split_columns(proj)
        y_a = _pool_mixer(p_pool, pool_w[l], pool_b[l], pool_scale[l])
        y_b = _spatial_gating(p_u, p_v, sgu_ln_g[l], sgu_ln_b[l], sgu_w[l], sgu_b[l])
        y_c = _short_conv_mixer(p_cb, p_cc, p_ch, sconv_w[l])
        y_d = _ssd_mixer(p_z, p_x, p_b, p_c, p_dt, ssd_conv_w[l], ssd_conv_b[l],
                         ssd_dt_bias[l], ssd_a_log[l], ssd_d[l], ssd_norm_g[l])
        mix = jnp.concatenate([y_a, y_b, y_c, y_d], axis=-1)
        h = h + jnp.einsum('bse,ed->bsd', mix, w_out[l])
        hn = _rmsnorm(h, norm_ffn_g[l])
        gate = jax.nn.silu(jnp.einsum('bsd,df->bsf', hn, w_gate[l]))
        up = jnp.einsum('bsd,df->bsf', hn, w_up[l])
        h = h + jnp.einsum('bsf,fd->bsd', gate * up, w_down[l])
    return _rmsnorm(h, final_norm_g)
```

```python
import functools

import jax
import jax.numpy as jnp
from jax import lax
from jax.experimental import pallas as pl
from jax.experimental.pallas import tpu as pltpu

F32 = jnp.float32
BF16 = jnp.bfloat16

D_MODEL = 1024
GROUP_W = 256
POOL_WINDOWS = (2, 4, 8, 16)
POOL_HALO = 16
SGU_BLOCK = 128
SGU_HEADS = 4
CONV_HALO = 8
SCONV_WIDTH = 3
SSD_CONV = 4
SSD_CHUNK = 64
SSD_STATE = 128
SSD_GROUPS = 2
SSD_XBC = 768
D_FF = 2816
EPS = 1e-6

C_POOL, C_U, C_V, C_CB, C_CC, C_CH, C_Z, C_XBC, C_DT = 0, 256, 512, 768, 1024, 1280, 1536, 1792, 2560
IN_COLS_PADDED = C_DT + GROUP_W

SEQ_TILE = 256
FFN_TILE = 512
FFN_CHUNK = 256
VMEM_LIMIT_BYTES = 56 * 1024 * 1024


def _dot(a, b):
    return jnp.dot(a, b, preferred_element_type=F32)


def _silu(x):
    return x * (1.0 / (1.0 + jnp.exp(-x)))


def _rms_scale(x):
    return x * lax.rsqrt(jnp.mean(x * x, axis=-1, keepdims=True) + EPS)


def _mixer_kernel(h_ref, g_ref, win_ref, poolw_ref, poolb_ref, pools_ref, lng_ref, lnb_ref,
                  sguw_ref, sgub_ref, sconvw_ref, convw_ref, convb_ref, dtb_ref, alog_ref,
                  dskip_ref, ng_ref, wout_ref, out_ref,
                  pool_buf, sc_buf, xbc_buf, state_ref, y_ref):
    ts = SEQ_TILE
    s_idx = pl.program_id(1)

    @pl.when(s_idx == 0)
    def _():
        pool_buf[0:POOL_HALO, :] = jnp.zeros((POOL_HALO, GROUP_W), F32)
        sc_buf[0:CONV_HALO, :] = jnp.zeros((CONV_HALO, GROUP_W), F32)
        xbc_buf[0:CONV_HALO, :] = jnp.zeros((CONV_HALO, SSD_XBC), F32)
        state_ref[...] = jnp.zeros(state_ref.shape, F32)

    h = h_ref[0]
    hn = (_rms_scale(h) * g_ref[...]).astype(BF16)

    def proj(c0, width):
        return _dot(hn, win_ref[:, c0:c0 + width])

    lane = lax.broadcasted_iota(jnp.int32, (1, 128), 1)
    lo_half = lane < 64

    xp = proj(C_POOL, GROUP_W)
    pool_buf[POOL_HALO:POOL_HALO + ts, :] = xp

    def lagged(k, c0):
        return pool_buf[POOL_HALO - k:POOL_HALO - k + ts, c0:c0 + 128]

    s2 = xp[:, 0:128] + lagged(1, 0)
    s4 = s2 + (lagged(2, 0) + lagged(3, 0))
    s8 = xp[:, 128:256] + lagged(1, 128)
    for k in range(2, 8):
        s8 = s8 + lagged(k, 128)
    s16 = s8
    for k in range(8, 16):
        s16 = s16 + lagged(k, 128)
    pos1 = (lax.broadcasted_iota(jnp.int32, (ts, 128), 0) + (s_idx * ts + 1)).astype(F32)
    cnt_a = jnp.where(lo_half, jnp.minimum(pos1, 2.0), jnp.minimum(pos1, 4.0))
    cnt_b = jnp.where(lo_half, jnp.minimum(pos1, 8.0), jnp.minimum(pos1, 16.0))
    pooled = jnp.concatenate([jnp.where(lo_half, s2, s4) / cnt_a,
                              jnp.where(lo_half, s8, s16) / cnt_b], axis=1) - xp
    y_a = (_dot(pooled.astype(BF16), poolw_ref[...]) + poolb_ref[...]) * pools_ref[...]
    pool_buf[0:POOL_HALO, :] = pool_buf[ts:ts + POOL_HALO, :]

    u = proj(C_U, GROUP_W)
    v = proj(C_V, GROUP_W)
    mu = jnp.mean(v, axis=-1, keepdims=True)
    vc = v - mu
    var = jnp.mean(vc * vc, axis=-1, keepdims=True)
    vn = ((vc * lax.rsqrt(var + EPS)) * lng_ref[...] + lnb_ref[...]).astype(BF16)
    ri = lax.broadcasted_iota(jnp.int32, (SGU_BLOCK, SGU_BLOCK), 0) // SSD_CHUNK
    ci = lax.broadcasted_iota(jnp.int32, (SGU_BLOCK, SGU_BLOCK), 1) // SSD_CHUNK
    chunk_causal = ri >= ci
    lane256 = lax.broadcasted_iota(jnp.int32, (1, GROUP_W), 1)
    w_heads = [jnp.where(chunk_causal, sguw_ref[hd], 0.0).astype(BF16) for hd in range(SGU_HEADS)]
    mixed_blocks = []
    for blk in range(ts // SGU_BLOCK):
        vb = vn[blk * SGU_BLOCK:(blk + 1) * SGU_BLOCK, :]
        mixed = _dot(w_heads[0], vb)
        for hd in range(1, SGU_HEADS):
            mixed = jnp.where(lane256 >= hd * 64, _dot(w_heads[hd], vb), mixed)
        mixed_blocks.append(mixed + sgub_ref[...])
    y_b = u * jnp.concatenate(mixed_blocks, axis=0)

    cb = proj(C_CB, GROUP_W)
    prod = proj(C_CC, GROUP_W) * proj(C_CH, GROUP_W)
    sc_buf[CONV_HALO:CONV_HALO + ts, :] = prod
    conv = prod * sconvw_ref[SCONV_WIDTH - 1:SCONV_WIDTH, :]
    for k in range(1, SCONV_WIDTH):
        conv = conv + sc_buf[CONV_HALO - k:CONV_HALO - k + ts, :] * sconvw_ref[SCONV_WIDTH - 1 - k:SCONV_WIDTH - k, :]
    y_c = cb * conv
    sc_buf[0:CONV_HALO, :] = sc_buf[ts:ts + CONV_HALO, :]

    z = proj(C_Z, GROUP_W)
    xbc_pre = proj(C_XBC, SSD_XBC)
    xbc_buf[CONV_HALO:CONV_HALO + ts, :] = xbc_pre
    conv = xbc_pre * convw_ref[SSD_CONV - 1:SSD_CONV, :] + convb_ref[...]
    for k in range(1, SSD_CONV):
        conv = conv + xbc_buf[CONV_HALO - k:CONV_HALO - k + ts, :] * convw_ref[SSD_CONV - 1 - k:SSD_CONV - k, :]
    xbc = _silu(conv)
    xbc_buf[0:CONV_HALO, :] = xbc_buf[ts:ts + CONV_HALO, :]
    xh = xbc[:, 0:GROUP_W]
    bmat = xbc[:, GROUP_W:2 * GROUP_W]
    cmat = xbc[:, 2 * GROUP_W:3 * GROUP_W]

    dt = proj(C_DT, GROUP_W) + dtb_ref[...]
    delta = jnp.maximum(dt, 0.0) + jnp.log1p(jnp.exp(-jnp.abs(dt)))
    da = delta * (-jnp.exp(alog_ref[...]))
    x_dt = xh * delta

    rr = lax.broadcasted_iota(jnp.int32, (ts, ts), 0)
    cc = lax.broadcasted_iota(jnp.int32, (ts, ts), 1)
    tri = jnp.where((cc <= rr) & ((rr // SSD_CHUNK) == (cc // SSD_CHUNK)), 1.0, 0.0).astype(BF16)
    da_hi = da.astype(BF16)
    rem = da - da_hi.astype(F32)
    da_mid = rem.astype(BF16)
    da_lo = (rem - da_mid.astype(F32)).astype(BF16)
    acs = _dot(tri, da_hi) + _dot(tri, da_mid) + _dot(tri, da_lo)

    row64 = lax.broadcasted_iota(jnp.int32, (SSD_CHUNK, 128), 0)
    col64 = lax.broadcasted_iota(jnp.int32, (SSD_CHUNK, 128), 1) % SSD_CHUNK
    causal2 = col64 <= row64
    for c in range(ts // SSD_CHUNK):
        r0 = c * SSD_CHUNK
        for g in range(SSD_GROUPS):
            c0 = g * 128
            b_g = bmat[r0:r0 + SSD_CHUNK, c0:c0 + 128]
            c_g = cmat[r0:r0 + SSD_CHUNK, c0:c0 + 128].astype(BF16)
            x2 = x_dt[r0:r0 + SSD_CHUNK, c0:c0 + 128]
            a2 = acs[r0:r0 + SSD_CHUNK, c0:c0 + 128]
            da2 = da[r0:r0 + SSD_CHUNK, c0:c0 + 128]
            last = a2[SSD_CHUNK - 1:SSD_CHUNK, :]
            a_key = jnp.sum(jnp.where(row64 <= col64, da2, 0.0), axis=0, keepdims=True)
            decay = jnp.exp(jnp.where(causal2, a2 - a_key, -jnp.inf))
            b_bf = b_g.astype(BF16)
            scores = lax.dot_general(c_g, jnp.concatenate([b_bf, b_bf], axis=0),
                                     (((1,), (1,)), ((), ())), preferred_element_type=F32)
            x_bd = jnp.concatenate([jnp.where(lo_half, x2, 0.0), jnp.where(lo_half, 0.0, x2)], axis=0)
            y_diag = _dot((scores * decay).astype(BF16), x_bd.astype(BF16))
            prev = state_ref[g]
            y_off = _dot(c_g, prev.astype(BF16)) * jnp.exp(a2)
            upd = lax.dot_general(b_bf, (x2 * jnp.exp(last - a2)).astype(BF16),
                                  (((0,), (0,)), ((), ())), preferred_element_type=F32)
            state_ref[g] = prev * jnp.exp(last) + upd
            y_ref[r0:r0 + SSD_CHUNK, c0:c0 + 128] = y_diag + y_off
    y = (y_ref[...] + xh * dskip_ref[...]) * _silu(z)
    y_d = jnp.concatenate([_rms_scale(y[:, 0:128]), _rms_scale(y[:, 128:256])], axis=1) * ng_ref[...]

    mix = jnp.concatenate([y_a, y_b, y_c, y_d], axis=1).astype(BF16)
    out_ref[0] = h + _dot(mix, wout_ref[...])


def _ffn_kernel(h_ref, g_ref, wg_ref, wu_ref, wd_ref, fg_ref, out_ref, hn_ref, acc_ref, *, final_norm):
    h = h_ref[...]
    hn_ref[...] = (_rms_scale(h) * g_ref[...]).astype(BF16)
    acc_ref[...] = h
    for j in range(D_FF // FFN_CHUNK):
        c0 = j * FFN_CHUNK
        hn = hn_ref[...]
        gate = _dot(hn, wg_ref[:, c0:c0 + FFN_CHUNK])
        up = _dot(hn, wu_ref[:, c0:c0 + FFN_CHUNK])
        act = (_silu(gate) * up).astype(BF16)
        acc_ref[...] += _dot(act, wd_ref[c0:c0 + FFN_CHUNK, :])
    res = acc_ref[...]
    if final_norm:
        res = _rms_scale(res) * fg_ref[...]
    out_ref[...] = res


def _const_spec(shape):
    zeros = (0,) * len(shape)
    return pl.BlockSpec(shape, lambda *_: zeros)


def _mixer_call(h, params):
    b, s, d = h.shape
    grid = (b, s // SEQ_TILE)
    tile_spec = pl.BlockSpec((1, SEQ_TILE, d), lambda i, j: (i, j, 0))
    in_specs = [tile_spec] + [_const_spec(p.shape) for p in params]
    return pl.pallas_call(
        _mixer_kernel,
        grid=grid,
        in_specs=in_specs,
        out_specs=tile_spec,
        out_shape=jax.ShapeDtypeStruct(h.shape, F32),
        scratch_shapes=[
            pltpu.VMEM((SEQ_TILE + POOL_HALO, GROUP_W), F32),
            pltpu.VMEM((SEQ_TILE + CONV_HALO, GROUP_W), F32),
            pltpu.VMEM((SEQ_TILE + CONV_HALO, SSD_XBC), F32),
            pltpu.VMEM((SSD_GROUPS, SSD_STATE, 128), F32),
            pltpu.VMEM((SEQ_TILE, GROUP_W), F32),
        ],
        compiler_params=pltpu.CompilerParams(
            dimension_semantics=("parallel", "arbitrary"),
            vmem_limit_bytes=VMEM_LIMIT_BYTES),
        name="mixer",
    )(h, *params)


def _ffn_call(h2d, g, wg, wu, wd, fg, final_norm):
    t, d = h2d.shape
    tile_spec = pl.BlockSpec((FFN_TILE, d), lambda i: (i, 0))
    params = (g, wg, wu, wd, fg)
    return pl.pallas_call(
        functools.partial(_ffn_kernel, final_norm=final_norm),
        grid=(t // FFN_TILE,),
        in_specs=[tile_spec] + [_const_spec(p.shape) for p in params],
        out_specs=tile_spec,
        out_shape=jax.ShapeDtypeStruct(h2d.shape, F32),
        scratch_shapes=[pltpu.VMEM((FFN_TILE, d), BF16), pltpu.VMEM((FFN_TILE, d), F32)],
        compiler_params=pltpu.CompilerParams(
            dimension_semantics=("parallel",),
            vmem_limit_bytes=VMEM_LIMIT_BYTES),
        name="ffn",
    )(h2d, *params)


def _row(v):
    return v.reshape(1, -1).astype(F32)


def _rep64(v):
    return jnp.repeat(v.astype(F32), 64).reshape(1, -1)


def kernel(x, norm_mix_g, w_in, pool_w, pool_b, pool_scale, sgu_ln_g, sgu_ln_b, sgu_w, sgu_b, sconv_w, ssd_conv_w, ssd_conv_b, ssd_dt_bias, ssd_a_log, ssd_d, ssd_norm_g, w_out, norm_ffn_g, w_gate, w_up, w_down, final_norm_g):
    b, s, d = x.shape
    depth = w_in.shape[0]
    h = x
    for l in range(depth):
        win = jnp.concatenate([w_in[l][:, :C_DT], jnp.repeat(w_in[l][:, C_DT:], 64, axis=1)], axis=1).astype(BF16)
        poolw = jax.scipy.linalg.block_diag(*[pool_w[l, g] for g in range(len(POOL_WINDOWS))]).astype(BF16)
        sgub = jnp.repeat(sgu_b[l].T, 64, axis=1)
        params = (
            _row(norm_mix_g[l]), win, poolw, _row(pool_b[l]), _row(pool_scale[l]),
            _row(sgu_ln_g[l]), _row(sgu_ln_b[l]), sgu_w[l], sgub, sconv_w[l],
            ssd_conv_w[l], _row(ssd_conv_b[l]), _rep64(ssd_dt_bias[l]), _rep64(ssd_a_log[l]),
            _rep64(ssd_d[l]), _row(ssd_norm_g[l]), w_out[l].astype(BF16),
        )
        h = _mixer_call(h, params)
        h = _ffn_call(h.reshape(b * s, d), _row(norm_ffn_g[l]), w_gate[l].astype(BF16),
                      w_up[l].astype(BF16), w_down[l].astype(BF16), _row(final_norm_g),
                      final_norm=(l == depth - 1)).reshape(b, s, d)
    return h
```

```python
import functools

import jax
import jax.numpy as jnp
from jax import lax
from jax.experimental import pallas as pl
from jax.experimental.pallas import tpu as pltpu

F32 = jnp.float32
BF16 = jnp.bfloat16

D_MODEL = 1024
GROUP_W = 256
POOL_WINDOWS = (2, 4, 8, 16)
POOL_HALO = 16
SGU_BLOCK = 128
SGU_HEADS = 4
CONV_HALO = 8
SCONV_WIDTH = 3
SSD_CONV = 4
SSD_CHUNK = 64
SSD_STATE = 128
SSD_GROUPS = 2
SSD_XBC = 768
D_FF = 2816
EPS = 1e-6

C_POOL, C_U, C_V, C_CB, C_CC, C_CH, C_Z, C_XBC, C_DT = 0, 256, 512, 768, 1024, 1280, 1536, 1792, 2560
IN_COLS_PADDED = C_DT + GROUP_W

SEQ_TILE = 1024
SUB_TILE = 512
N_SUB = SEQ_TILE // SUB_TILE
assert N_SUB == 2
FFN_TILE = 512
FFN_CHUNK = 256
VMEM_LIMIT_BYTES = 56 * 1024 * 1024
PROJ_PIECES_AFTER_MIX_PIECE = (1, 1, 1, 1, 1, 1, 1, 1, 0, 1, 1, 0, 1, 1, 0, 0)


def _dot(a, b):
    return jnp.dot(a, b, preferred_element_type=F32)


def _silu(x):
    return x * (1.0 / (1.0 + jnp.exp(-x)))


def _rms_scale(x):
    return x * lax.rsqrt(jnp.mean(x * x, axis=-1, keepdims=True) + EPS)


def _proj_stage(h_ref, row0, g_ref, win_ref, hn_buf, proj_buf):
    hn_buf[...] = (_rms_scale(h_ref[row0:row0 + SUB_TILE, :]) * g_ref[...]).astype(BF16)
    yield
    for c0 in range(0, IN_COLS_PADDED, GROUP_W):
        proj_buf[:, c0:c0 + GROUP_W] = _dot(hn_buf[...], win_ref[:, c0:c0 + GROUP_W])
        yield


def _mix_stage(k, pos0, h_ref, row0, out_ref, proj, prm, carry):
    (poolw_ref, poolb_ref, pools_ref, lng_ref, lnb_ref, sguw_ref, sgub_ref, sconvw_ref,
     convw_ref, convb_ref, dtb_ref, alog_ref, dskip_ref, ng_ref, wout_ref) = prm
    pool_buf, sc_buf, xbc_buf, state_ref = carry
    ts = SUB_TILE
    nxt = (k + 1) % N_SUB
    rows = slice(row0, row0 + ts)

    def add_out(m, y, first=False):
        d = _dot(y.astype(BF16), wout_ref[m * GROUP_W:(m + 1) * GROUP_W, :])
        if first:
            out_ref[rows, :] = h_ref[rows, :] + d
        else:
            out_ref[rows, :] += d

    lane = lax.broadcasted_iota(jnp.int32, (1, 128), 1)
    lo_half = lane < 64

    v = proj[:, C_V:C_V + GROUP_W]
    mu = jnp.mean(v, axis=-1, keepdims=True)
    vc = v - mu
    var = jnp.mean(vc * vc, axis=-1, keepdims=True)
    vn = ((vc * lax.rsqrt(var + EPS)) * lng_ref[...] + lnb_ref[...]).astype(BF16)
    ri = lax.broadcasted_iota(jnp.int32, (SGU_BLOCK, SGU_BLOCK), 0) // SSD_CHUNK
    ci = lax.broadcasted_iota(jnp.int32, (SGU_BLOCK, SGU_BLOCK), 1) // SSD_CHUNK
    chunk_causal = ri >= ci
    lane256 = lax.broadcasted_iota(jnp.int32, (1, GROUP_W), 1)
    w_heads = [jnp.where(chunk_causal, sguw_ref[hd], 0.0).astype(BF16) for hd in range(SGU_HEADS)]
    mixed_blocks = []
    for blk in range(ts // SGU_BLOCK):
        vb = vn[blk * SGU_BLOCK:(blk + 1) * SGU_BLOCK, :]
        mixed = _dot(w_heads[0], vb)
        for hd in range(1, SGU_HEADS):
            mixed = jnp.where(lane256 >= hd * 64, _dot(w_heads[hd], vb), mixed)
        mixed_blocks.append(mixed + sgub_ref[...])
    y_b = proj[:, C_U:C_U + GROUP_W] * jnp.concatenate(mixed_blocks, axis=0)
    add_out(1, y_b, first=True)
    yield

    xp = proj[:, C_POOL:C_POOL + GROUP_W]
    pool_buf[k, POOL_HALO:POOL_HALO + ts, :] = xp
    pool_buf[nxt, 0:POOL_HALO, :] = xp[ts - POOL_HALO:ts, :]

    def lagged(lag, c0):
        return pool_buf[k, POOL_HALO - lag:POOL_HALO - lag + ts, c0:c0 + 128]

    s2 = xp[:, 0:128] + lagged(1, 0)
    s4 = s2 + (lagged(2, 0) + lagged(3, 0))
    s8 = xp[:, 128:256] + lagged(1, 128)
    for lag in range(2, 8):
        s8 = s8 + lagged(lag, 128)
    s16 = s8
    for lag in range(8, 16):
        s16 = s16 + lagged(lag, 128)
    pos1 = (lax.broadcasted_iota(jnp.int32, (ts, 128), 0) + (pos0 + 1)).astype(F32)
    cnt_a = jnp.where(lo_half, jnp.minimum(pos1, 2.0), jnp.minimum(pos1, 4.0))
    cnt_b = jnp.where(lo_half, jnp.minimum(pos1, 8.0), jnp.minimum(pos1, 16.0))
    pooled = jnp.concatenate([jnp.where(lo_half, s2, s4) / cnt_a,
                              jnp.where(lo_half, s8, s16) / cnt_b], axis=1) - xp
    y_a = (_dot(pooled.astype(BF16), poolw_ref[...]) + poolb_ref[...]) * pools_ref[...]
    add_out(0, y_a)
    yield

    prod = proj[:, C_CC:C_CC + GROUP_W] * proj[:, C_CH:C_CH + GROUP_W]
    sc_buf[k, CONV_HALO:CONV_HALO + ts, :] = prod
    sc_buf[nxt, 0:CONV_HALO, :] = prod[ts - CONV_HALO:ts, :]
    conv = prod * sconvw_ref[SCONV_WIDTH - 1:SCONV_WIDTH, :]
    for lag in range(1, SCONV_WIDTH):
        conv = conv + (sc_buf[k, CONV_HALO - lag:CONV_HALO - lag + ts, :]
                       * sconvw_ref[SCONV_WIDTH - 1 - lag:SCONV_WIDTH - lag, :])
    y_c = proj[:, C_CB:C_CB + GROUP_W] * conv
    add_out(2, y_c)
    yield

    xbc_buf[nxt, 0:CONV_HALO, :] = proj[ts - CONV_HALO:ts, C_XBC:C_XBC + SSD_XBC]
    xbc = []
    for part in range(SSD_XBC // GROUP_W):
        cols = slice(part * GROUP_W, (part + 1) * GROUP_W)
        pre = proj[:, C_XBC + part * GROUP_W:C_XBC + (part + 1) * GROUP_W]
        xbc_buf[k, CONV_HALO:CONV_HALO + ts, cols] = pre
        conv = pre * convw_ref[SSD_CONV - 1:SSD_CONV, cols] + convb_ref[:, cols]
        for lag in range(1, SSD_CONV):
            conv = conv + (xbc_buf[k, CONV_HALO - lag:CONV_HALO - lag + ts, cols]
                           * convw_ref[SSD_CONV - 1 - lag:SSD_CONV - lag, cols])
        xbc.append(_silu(conv))
        yield
    xh, bmat, cmat = xbc

    dt = proj[:, C_DT:C_DT + GROUP_W] + dtb_ref[...]
    delta = jnp.maximum(dt, 0.0) + jnp.log1p(jnp.exp(-jnp.abs(dt)))
    da = delta * (-jnp.exp(alog_ref[...]))
    x_dt = xh * delta

    rr = lax.broadcasted_iota(jnp.int32, (SSD_CHUNK, 3 * SSD_CHUNK), 0)
    cc = lax.broadcasted_iota(jnp.int32, (SSD_CHUNK, 3 * SSD_CHUNK), 1) % SSD_CHUNK
    tri3 = jnp.where(cc <= rr, 1.0, 0.0).astype(BF16)
    da_hi = da.astype(BF16)
    rem = da - da_hi.astype(F32)
    da_mid = rem.astype(BF16)
    da_lo = (rem - da_mid.astype(F32)).astype(BF16)
    acs_chunks = []
    for c in range(ts // SSD_CHUNK):
        crow = slice(c * SSD_CHUNK, (c + 1) * SSD_CHUNK)
        acs_chunks.append(_dot(tri3, jnp.concatenate([da_hi[crow], da_mid[crow], da_lo[crow]], axis=0)))
    acs = jnp.concatenate(acs_chunks, axis=0)
    yield

    row64 = lax.broadcasted_iota(jnp.int32, (SSD_CHUNK, 128), 0)
    col64 = lax.broadcasted_iota(jnp.int32, (SSD_CHUNK, 128), 1) % SSD_CHUNK
    causal2 = col64 <= row64
    states = [state_ref[g] for g in range(SSD_GROUPS)]
    first_head_rows = lax.broadcasted_iota(jnp.int32, (128, 128), 0) < 64
    y_rows = []
    for c in range(ts // SSD_CHUNK):
        r0 = c * SSD_CHUNK
        y_cols = []
        for g in range(SSD_GROUPS):
            c0 = g * 128
            b_bf = bmat[r0:r0 + SSD_CHUNK, c0:c0 + 128].astype(BF16)
            c_g = cmat[r0:r0 + SSD_CHUNK, c0:c0 + 128].astype(BF16)
            x2 = x_dt[r0:r0 + SSD_CHUNK, c0:c0 + 128]
            a2 = acs[r0:r0 + SSD_CHUNK, c0:c0 + 128]
            da2 = da[r0:r0 + SSD_CHUNK, c0:c0 + 128]
            last = a2[SSD_CHUNK - 1:SSD_CHUNK, :]
            a_key = jnp.sum(jnp.where(row64 <= col64, da2, 0.0), axis=0, keepdims=True)
            decay = jnp.exp(jnp.where(causal2, a2 - a_key, -jnp.inf))
            rhs = jnp.concatenate([b_bf, b_bf, states[g].astype(BF16)], axis=0)
            both = lax.dot_general(c_g, rhs, (((1,), (1,)), ((), ())), preferred_element_type=F32)
            x_bd = jnp.concatenate([jnp.where(lo_half, x2, 0.0), jnp.where(lo_half, 0.0, x2)], axis=0)
            y_diag = _dot((both[:, 0:128] * decay).astype(BF16), x_bd.astype(BF16))
            y_off = both[:, 128:256] * jnp.exp(a2)
            upd = lax.dot_general((x2 * jnp.exp(last - a2)).astype(BF16), b_bf,
                                  (((0,), (0,)), ((), ())), preferred_element_type=F32)
            chunk_decay = jnp.exp(jnp.where(first_head_rows, last[:, 0:1], last[:, 127:128]))
            states[g] = states[g] * chunk_decay + upd
            y_cols.append(y_diag + y_off)
        y_rows.append(jnp.concatenate(y_cols, axis=1))
        yield
    for g in range(SSD_GROUPS):
        state_ref[g] = states[g]
    y = (jnp.concatenate(y_rows, axis=0) + xh * dskip_ref[...]) * _silu(proj[:, C_Z:C_Z + GROUP_W])
    y_d = jnp.concatenate([_rms_scale(y[:, 0:128]), _rms_scale(y[:, 128:256])], axis=1) * ng_ref[...]
    add_out(3, y_d)
    yield


def _interleave(main, side):
    for n_side in PROJ_PIECES_AFTER_MIX_PIECE:
        next(main)
        for _ in range(n_side):
            next(side)
    assert next(main, "done") == "done" and next(side, "done") == "done"


N_MIXER_PARAMS = 17


def _mixer_kernel(h_ref, hnext_ref, *refs, tiles_per_seq):
    g_ref, win_ref = refs[0:2]
    prm = refs[2:N_MIXER_PARAMS]
    out_ref = refs[N_MIXER_PARAMS]
    hn_buf, proj_a, proj_b, pool_buf, sc_buf, xbc_buf, state_ref = refs[N_MIXER_PARAMS + 1:]
    carry = (pool_buf, sc_buf, xbc_buf, state_ref)
    step = pl.program_id(0)
    tile_in_seq = step % tiles_per_seq

    @pl.when(step == 0)
    def _():
        for _ in _proj_stage(h_ref, 0, g_ref, win_ref, hn_buf, proj_a):
            pass

    @pl.when(tile_in_seq == 0)
    def _():
        pool_buf[0, 0:POOL_HALO, :] = jnp.zeros((POOL_HALO, GROUP_W), F32)
        sc_buf[0, 0:CONV_HALO, :] = jnp.zeros((CONV_HALO, GROUP_W), F32)
        xbc_buf[0, 0:CONV_HALO, :] = jnp.zeros((CONV_HALO, SSD_XBC), F32)
        state_ref[...] = jnp.zeros(state_ref.shape, F32)

    pos0 = tile_in_seq * SEQ_TILE
    _interleave(_mix_stage(0, pos0, h_ref, 0, out_ref, proj_a, prm, carry),
                _proj_stage(h_ref, SUB_TILE, g_ref, win_ref, hn_buf, proj_b))
    _interleave(_mix_stage(1, pos0 + SUB_TILE, h_ref, SUB_TILE, out_ref, proj_b, prm, carry),
                _proj_stage(hnext_ref, 0, g_ref, win_ref, hn_buf, proj_a))


def _ffn_kernel(h_ref, g_ref, wg_ref, wu_ref, wd_ref, fg_ref, out_ref, hn_ref, acc_ref, *, final_norm):
    h = h_ref[...]
    hn_ref[...] = (_rms_scale(h) * g_ref[...]).astype(BF16)
    acc_ref[...] = h
    for j in range(D_FF // FFN_CHUNK):
        c0 = j * FFN_CHUNK
        hn = hn_ref[...]
        gate = _dot(hn, wg_ref[:, c0:c0 + FFN_CHUNK])
        up = _dot(hn, wu_ref[:, c0:c0 + FFN_CHUNK])
        act = (_silu(gate) * up).astype(BF16)
        acc_ref[...] += _dot(act, wd_ref[c0:c0 + FFN_CHUNK, :])
    res = acc_ref[...]
    if final_norm:
        res = _rms_scale(res) * fg_ref[...]
    out_ref[...] = res


def _const_spec(shape):
    zeros = (0,) * len(shape)
    return pl.BlockSpec(shape, lambda *_: zeros)


def _mixer_call(h2d, seq_len, params):
    assert len(params) == N_MIXER_PARAMS
    t, d = h2d.shape
    n_tiles = t // SEQ_TILE
    assert seq_len % SEQ_TILE == 0 and t % seq_len == 0
    tile_spec = pl.BlockSpec((SEQ_TILE, d), lambda i: (i, 0))
    next_spec = pl.BlockSpec((SUB_TILE, d), lambda i: (jnp.minimum(i + 1, n_tiles - 1) * N_SUB, 0))
    in_specs = [tile_spec, next_spec] + [_const_spec(p.shape) for p in params]
    return pl.pallas_call(
        functools.partial(_mixer_kernel, tiles_per_seq=seq_len // SEQ_TILE),
        grid=(n_tiles,),
        in_specs=in_specs,
        out_specs=tile_spec,
        out_shape=jax.ShapeDtypeStruct(h2d.shape, F32),
        scratch_shapes=[
            pltpu.VMEM((SUB_TILE, d), BF16),
            pltpu.VMEM((SUB_TILE, IN_COLS_PADDED), F32),
            pltpu.VMEM((SUB_TILE, IN_COLS_PADDED), F32),
            pltpu.VMEM((N_SUB, SUB_TILE + POOL_HALO, GROUP_W), F32),
            pltpu.VMEM((N_SUB, SUB_TILE + CONV_HALO, GROUP_W), F32),
            pltpu.VMEM((N_SUB, SUB_TILE + CONV_HALO, SSD_XBC), F32),
            pltpu.VMEM((SSD_GROUPS, SSD_STATE, 128), F32),
        ],
        compiler_params=pltpu.CompilerParams(
            dimension_semantics=("arbitrary",),
            vmem_limit_bytes=VMEM_LIMIT_BYTES),
        name="mixer",
    )(h2d, h2d, *params)


def _ffn_call(h2d, g, wg, wu, wd, fg, final_norm):
    t, d = h2d.shape
    tile_spec = pl.BlockSpec((FFN_TILE, d), lambda i: (i, 0))
    params = (g, wg, wu, wd, fg)
    return pl.pallas_call(
        functools.partial(_ffn_kernel, final_norm=final_norm),
        grid=(t // FFN_TILE,),
        in_specs=[tile_spec] + [_const_spec(p.shape) for p in params],
        out_specs=tile_spec,
        out_shape=jax.ShapeDtypeStruct(h2d.shape, F32),
        scratch_shapes=[pltpu.VMEM((FFN_TILE, d), BF16), pltpu.VMEM((FFN_TILE, d), F32)],
        compiler_params=pltpu.CompilerParams(
            dimension_semantics=("parallel",),
            vmem_limit_bytes=VMEM_LIMIT_BYTES),
        name="ffn",
    )(h2d, *params)


def _row(v):
    return v.reshape(1, -1).astype(F32)


def _rep64(v):
    return jnp.repeat(v.astype(F32), 64).reshape(1, -1)


def kernel(x, norm_mix_g, w_in, pool_w, pool_b, pool_scale, sgu_ln_g, sgu_ln_b, sgu_w, sgu_b, sconv_w, ssd_conv_w, ssd_conv_b, ssd_dt_bias, ssd_a_log, ssd_d, ssd_norm_g, w_out, norm_ffn_g, w_gate, w_up, w_down, final_norm_g):
    b, s, d = x.shape
    depth = w_in.shape[0]
    h = x.reshape(b * s, d)
    for l in range(depth):
        win = jnp.concatenate([w_in[l][:, :C_DT], jnp.repeat(w_in[l][:, C_DT:], 64, axis=1)], axis=1).astype(BF16)
        poolw = jax.scipy.linalg.block_diag(*[pool_w[l, g] for g in range(len(POOL_WINDOWS))]).astype(BF16)
        sgub = jnp.repeat(sgu_b[l].T, 64, axis=1)
        params = (
            _row(norm_mix_g[l]), win, poolw, _row(pool_b[l]), _row(pool_scale[l]),
            _row(sgu_ln_g[l]), _row(sgu_ln_b[l]), sgu_w[l], sgub, sconv_w[l],
            ssd_conv_w[l], _row(ssd_conv_b[l]), _rep64(ssd_dt_bias[l]), _rep64(ssd_a_log[l]),
            _rep64(ssd_d[l]), _row(ssd_norm_g[l]), w_out[l].astype(BF16),
        )
        h = _mixer_call(h, s, params)
        h = _ffn_call(h, _row(norm_ffn_g[l]), w_gate[l].astype(BF16), w_up[l].astype(BF16),
                      w_down[l].astype(BF16), _row(final_norm_g), final_norm=(l == depth - 1))
    return h.reshape(b, s, d)
```

```python
import functools

import jax
import jax.numpy as jnp
from jax import lax
from jax.experimental import pallas as pl
from jax.experimental.pallas import tpu as pltpu

F32 = jnp.float32
BF16 = jnp.bfloat16

D_MODEL = 1024
GROUP_W = 256
POOL_GROUPS = 4
SGU_BLOCK = 128
SGU_HEADS = 4
HALO = 8
SCONV_WIDTH = 3
SSD_CONV = 4
SSD_CHUNK = 64
SSD_STATE = 128
SSD_GROUPS = 2
SSD_HEADS = 4
SSD_XBC = 768
D_FF = 2816
EPS = 1e-6

C_POOL, C_U, C_V, C_CB, C_CC, C_CH, C_Z, C_XBC, C_DT = 0, 256, 512, 768, 1024, 1280, 1536, 1792, 2560
PROJ_COLS = C_DT + GROUP_W

V_NORM_G, V_POOL_SGU, V_SSD, V_CONV_B, V_SCONV_W, V_CONV_W = 0, 1, 2, 3, 4, 5
N_VEC_ROWS = V_CONV_W + SSD_CONV

SEQ_TILE = 1024
SUB_TILE = 512
N_SUB = SEQ_TILE // SUB_TILE
assert N_SUB == 2
FFN_TILE = 512
FFN_CHUNK = 256
VMEM_LIMIT_BYTES = 56 * 1024 * 1024
PROJ_PIECES_AFTER_MIX_PIECE = (2, 1, 1, 1, 1, 1, 1, 1, 1, 1, 1, 0)


def _dot(a, b):
    return jnp.dot(a, b, preferred_element_type=F32)


def _silu(x):
    return x * (1.0 / (1.0 + jnp.exp(-x)))


def _rms_scale(x):
    return x * lax.rsqrt(jnp.mean(x * x, axis=-1, keepdims=True) + EPS)


def _proj_stage(h_ref, row0, vec_ref, win_ref, wdt_ref, hn_buf, proj_buf):
    g = vec_ref[V_NORM_G:V_NORM_G + 1, :]
    hn_buf[...] = (_rms_scale(h_ref[row0:row0 + SUB_TILE, :]) * g).astype(BF16)
    yield
    for c0 in range(0, C_DT, GROUP_W):
        proj_buf[:, c0:c0 + GROUP_W] = _dot(hn_buf[...], win_ref[:, c0:c0 + GROUP_W])
        yield
    proj_buf[:, C_DT:C_DT + GROUP_W] = _dot(hn_buf[...], wdt_ref[...])
    yield


def _mix_stage(k, pos0, h_ref, row0, out_ref, proj, prm, carry):
    vec_ref, poolw_ref, sguw_ref, sgub_ref, wout_ref = prm
    x_buf, s2_buf, s4_buf, s8_buf, sc_buf, xbc_buf, state_ref = carry
    ts = SUB_TILE
    nxt = (k + 1) % N_SUB
    rows = slice(row0, row0 + ts)
    n_chunks = ts // SSD_CHUNK

    def vec(row, slot, width=GROUP_W):
        return vec_ref[row:row + 1, slot * GROUP_W:slot * GROUP_W + width]

    def out_dot(m, y):
        return _dot(y.astype(BF16), wout_ref[m * GROUP_W:(m + 1) * GROUP_W, :])

    def with_halo(buf, val):
        buf[k, HALO:HALO + ts, :] = val
        buf[nxt, 0:HALO, :] = val[ts - HALO:ts, :]

    lane = lax.broadcasted_iota(jnp.int32, (1, 128), 1)
    lo_half = lane < 64

    xbc_buf[nxt, 0:HALO, :] = proj[ts - HALO:ts, C_XBC:C_XBC + SSD_XBC]
    xbc = []
    for part in range(SSD_XBC // GROUP_W):
        cols = slice(part * GROUP_W, (part + 1) * GROUP_W)
        pre = proj[:, C_XBC + part * GROUP_W:C_XBC + (part + 1) * GROUP_W]
        xbc_buf[k, HALO:HALO + ts, cols] = pre
        conv = pre * vec_ref[V_CONV_W + SSD_CONV - 1:V_CONV_W + SSD_CONV, cols] + vec_ref[V_CONV_B:V_CONV_B + 1, cols]
        for lag in range(1, SSD_CONV):
            tap = V_CONV_W + SSD_CONV - 1 - lag
            conv = conv + xbc_buf[k, HALO - lag:HALO - lag + ts, cols] * vec_ref[tap:tap + 1, cols]
        xbc.append(_silu(conv))
        yield
    xh, bmat, cmat = xbc

    dt = proj[:, C_DT:C_DT + GROUP_W] + vec(V_SSD, 0)
    delta = jnp.maximum(dt, 0.0) + jnp.log(1.0 + jnp.exp(-jnp.abs(dt)))
    da = delta * (-jnp.exp(vec(V_SSD, 1)))
    x_dt = xh * delta

    rr = lax.broadcasted_iota(jnp.int32, (SSD_CHUNK, 3 * SSD_CHUNK), 0)
    cc = lax.broadcasted_iota(jnp.int32, (SSD_CHUNK, 3 * SSD_CHUNK), 1) % SSD_CHUNK
    tri3 = jnp.where(cc <= rr, 1.0, 0.0).astype(BF16)
    da_hi = da.astype(BF16)
    rem = da - da_hi.astype(F32)
    da_mid = rem.astype(BF16)
    da_lo = (rem - da_mid.astype(F32)).astype(BF16)
    acs_chunks = []
    for c in range(n_chunks):
        crow = slice(c * SSD_CHUNK, (c + 1) * SSD_CHUNK)
        acs_chunks.append(_dot(tri3, jnp.concatenate([da_hi[crow], da_mid[crow], da_lo[crow]], axis=0)))
    yield

    v = proj[:, C_V:C_V + GROUP_W]
    mu = jnp.mean(v, axis=-1, keepdims=True)
    vc = v - mu
    var = jnp.mean(vc * vc, axis=-1, keepdims=True)
    vn = ((vc * lax.rsqrt(var + EPS)) * vec(V_POOL_SGU, 2) + vec(V_POOL_SGU, 3)).astype(BF16)
    ri = lax.broadcasted_iota(jnp.int32, (SGU_BLOCK, SGU_BLOCK), 0) // SSD_CHUNK
    ci = lax.broadcasted_iota(jnp.int32, (SGU_BLOCK, SGU_BLOCK), 1) // SSD_CHUNK
    chunk_causal = ri >= ci
    lane256 = lax.broadcasted_iota(jnp.int32, (1, GROUP_W), 1)
    w_heads = [jnp.where(chunk_causal, sguw_ref[hd], 0.0).astype(BF16) for hd in range(SGU_HEADS)]
    head_mix = [[_dot(w_heads[hd], vn[blk * SGU_BLOCK:(blk + 1) * SGU_BLOCK, :]) for hd in range(SGU_HEADS)]
                for blk in range(ts // SGU_BLOCK)]
    yield

    row64 = lax.broadcasted_iota(jnp.int32, (SSD_CHUNK, 128), 0)
    col64 = lax.broadcasted_iota(jnp.int32, (SSD_CHUNK, 128), 1) % SSD_CHUNK
    causal2 = col64 <= row64
    first_head_rows = lax.broadcasted_iota(jnp.int32, (128, 128), 0) < 64
    blocks = [(c, g) for c in range(n_chunks) for g in range(SSD_GROUPS)]

    def blk(arr, c, g):
        return arr[c * SSD_CHUNK:(c + 1) * SSD_CHUNK, g * 128:(g + 1) * 128]

    a2, b_bf, upd = {}, {}, {}
    for c, g in blocks:
        a2[c, g] = acs_chunks[c][:, g * 128:(g + 1) * 128]
        b_bf[c, g] = blk(bmat, c, g).astype(BF16)
        last = a2[c, g][SSD_CHUNK - 1:SSD_CHUNK, :]
        upd[c, g] = lax.dot_general((blk(x_dt, c, g) * jnp.exp(last - a2[c, g])).astype(BF16), b_bf[c, g],
                                    (((0,), (0,)), ((), ())), preferred_element_type=F32)
    yield

    xp = proj[:, C_POOL:C_POOL + GROUP_W]
    with_halo(x_buf, xp)
    s2 = xp + x_buf[k, HALO - 1:HALO - 1 + ts, :]
    with_halo(s2_buf, s2)
    s4 = s2 + s2_buf[k, HALO - 2:HALO - 2 + ts, :]
    with_halo(s4_buf, s4[:, 128:256])
    s8 = s4[:, 128:256] + s4_buf[k, HALO - 4:HALO - 4 + ts, :]
    with_halo(s8_buf, s8)
    s16 = s8 + s8_buf[k, 0:ts, :]
    pos1 = (lax.broadcasted_iota(jnp.int32, (ts, 128), 0) + (pos0 + 1)).astype(F32)
    cnt_a = jnp.where(lo_half, jnp.minimum(pos1, 2.0), jnp.minimum(pos1, 4.0))
    cnt_b = jnp.where(lo_half, jnp.minimum(pos1, 8.0), jnp.minimum(pos1, 16.0))
    pooled = jnp.concatenate([jnp.where(lo_half, s2[:, 0:128], s4[:, 0:128]) / cnt_a,
                              jnp.where(lo_half, s8, s16) / cnt_b], axis=1) - xp
    pool_mm = _dot(pooled.astype(BF16), poolw_ref[...])

    mixed_blocks = []
    for per_head in head_mix:
        mixed = per_head[0]
        for hd in range(1, SGU_HEADS):
            mixed = jnp.where(lane256 >= hd * 64, per_head[hd], mixed)
        mixed_blocks.append(mixed + sgub_ref[...])
    y_b = proj[:, C_U:C_U + GROUP_W] * jnp.concatenate(mixed_blocks, axis=0)
    d_b = out_dot(1, y_b)
    yield

    states = [state_ref[g] for g in range(SSD_GROUPS)]
    both = {}
    for c, g in blocks:
        rhs = jnp.concatenate([b_bf[c, g], b_bf[c, g], states[g].astype(BF16)], axis=0)
        both[c, g] = lax.dot_general(blk(cmat, c, g).astype(BF16), rhs, (((1,), (1,)), ((), ())),
                                     preferred_element_type=F32)
        last = a2[c, g][SSD_CHUNK - 1:SSD_CHUNK, :]
        chunk_decay = jnp.exp(jnp.where(first_head_rows, last[:, 0:1], last[:, 127:128]))
        states[g] = states[g] * chunk_decay + upd[c, g]
    for g in range(SSD_GROUPS):
        state_ref[g] = states[g]
    yield

    prod = proj[:, C_CC:C_CC + GROUP_W] * proj[:, C_CH:C_CH + GROUP_W]
    with_halo(sc_buf, prod)
    conv = prod * vec(V_SCONV_W, SCONV_WIDTH - 1)
    for lag in range(1, SCONV_WIDTH):
        conv = conv + sc_buf[k, HALO - lag:HALO - lag + ts, :] * vec(V_SCONV_W, SCONV_WIDTH - 1 - lag)
    y_c = proj[:, C_CB:C_CB + GROUP_W] * conv
    d_c = out_dot(2, y_c)
    y_a = (pool_mm + vec(V_POOL_SGU, 0)) * vec(V_POOL_SGU, 1)
    d_a = out_dot(0, y_a)
    out_ref[rows, :] = h_ref[rows, :] + d_b
    yield

    y_diag = {}
    for c, g in blocks:
        a_key = jnp.sum(jnp.where(row64 <= col64, blk(da, c, g), 0.0), axis=0, keepdims=True)
        decay = jnp.exp(jnp.where(causal2, a2[c, g] - a_key, -jnp.inf))
        x2 = blk(x_dt, c, g)
        x_bd = jnp.concatenate([jnp.where(lo_half, x2, 0.0), jnp.where(lo_half, 0.0, x2)], axis=0)
        y_diag[c, g] = _dot((both[c, g][:, 0:128] * decay).astype(BF16), x_bd.astype(BF16))
    out_ref[rows, :] += d_a + d_c
    yield

    y_ssd = jnp.concatenate(
        [jnp.concatenate([y_diag[c, g] + both[c, g][:, 128:256] * jnp.exp(a2[c, g]) for g in range(SSD_GROUPS)], axis=1)
         for c in range(n_chunks)], axis=0)
    y = (y_ssd + xh * vec(V_SSD, 2)) * _silu(proj[:, C_Z:C_Z + GROUP_W])
    y_d = jnp.concatenate([_rms_scale(y[:, 0:128]), _rms_scale(y[:, 128:256])], axis=1) * vec(V_SSD, 3)
    d_d = out_dot(3, y_d)
    yield
    out_ref[rows, :] += d_d
    yield


def _interleave(main, side):
    for n_side in PROJ_PIECES_AFTER_MIX_PIECE:
        next(main)
        for _ in range(n_side):
            next(side)
    assert next(main, "done") == "done" and next(side, "done") == "done"


def _mixer_kernel(h_ref, hnext_ref, vec_ref, win_ref, wdt_ref, poolw_ref, sguw_ref, sgub_ref, wout_ref,
                  out_ref, hn_buf, proj_a, proj_b, x_buf, s2_buf, s4_buf, s8_buf, sc_buf, xbc_buf,
                  state_ref, *, tiles_per_seq):
    prm = (vec_ref, poolw_ref, sguw_ref, sgub_ref, wout_ref)
    halo_bufs = (x_buf, s2_buf, s4_buf, s8_buf, sc_buf, xbc_buf)
    carry = halo_bufs + (state_ref,)
    step = pl.program_id(0)
    tile_in_seq = step % tiles_per_seq

    def proj_stage(src_ref, row0, dst):
        return _proj_stage(src_ref, row0, vec_ref, win_ref, wdt_ref, hn_buf, dst)

    @pl.when(step == 0)
    def _():
        for _ in proj_stage(h_ref, 0, proj_a):
            pass

    @pl.when(tile_in_seq == 0)
    def _():
        for buf in halo_bufs:
            buf[0, 0:HALO, :] = jnp.zeros((HALO, buf.shape[2]), F32)
        state_ref[...] = jnp.zeros(state_ref.shape, F32)

    pos0 = tile_in_seq * SEQ_TILE
    _interleave(_mix_stage(0, pos0, h_ref, 0, out_ref, proj_a, prm, carry),
                proj_stage(h_ref, SUB_TILE, proj_b))
    _interleave(_mix_stage(1, pos0 + SUB_TILE, h_ref, SUB_TILE, out_ref, proj_b, prm, carry),
                proj_stage(hnext_ref, 0, proj_a))


def _ffn_kernel(h_ref, g_ref, wg_ref, wu_ref, wd_ref, fg_ref, out_ref, hn_ref, acc_ref, *, final_norm):
    h = h_ref[...]
    hn_ref[...] = (_rms_scale(h) * g_ref[...]).astype(BF16)
    acc_ref[...] = h
    for j in range(D_FF // FFN_CHUNK):
        c0 = j * FFN_CHUNK
        hn = hn_ref[...]
        gate = _dot(hn, wg_ref[:, c0:c0 + FFN_CHUNK])
        up = _dot(hn, wu_ref[:, c0:c0 + FFN_CHUNK])
        act = (_silu(gate) * up).astype(BF16)
        acc_ref[...] += _dot(act, wd_ref[c0:c0 + FFN_CHUNK, :])
    res = acc_ref[...]
    if final_norm:
        res = _rms_scale(res) * fg_ref[...]
    out_ref[...] = res


def _layer_spec(arr, layer):
    zeros = (0,) * (arr.ndim - 1)
    return pl.BlockSpec((None,) + arr.shape[1:], lambda i: (layer,) + zeros)


def _mixer_call(h2d, seq_len, layer, params):
    t, d = h2d.shape
    n_tiles = t // SEQ_TILE
    assert seq_len % SEQ_TILE == 0 and t % seq_len == 0
    tile_spec = pl.BlockSpec((SEQ_TILE, d), lambda i: (i, 0))
    next_spec = pl.BlockSpec((SUB_TILE, d), lambda i: (jnp.minimum(i + 1, n_tiles - 1) * N_SUB, 0))
    return pl.pallas_call(
        functools.partial(_mixer_kernel, tiles_per_seq=seq_len // SEQ_TILE),
        grid=(n_tiles,),
        in_specs=[tile_spec, next_spec] + [_layer_spec(p, layer) for p in params],
        out_specs=tile_spec,
        out_shape=jax.ShapeDtypeStruct(h2d.shape, F32),
        scratch_shapes=[
            pltpu.VMEM((SUB_TILE, d), BF16),
            pltpu.VMEM((SUB_TILE, PROJ_COLS), F32),
            pltpu.VMEM((SUB_TILE, PROJ_COLS), F32),
            pltpu.VMEM((N_SUB, SUB_TILE + HALO, GROUP_W), F32),
            pltpu.VMEM((N_SUB, SUB_TILE + HALO, GROUP_W), F32),
            pltpu.VMEM((N_SUB, SUB_TILE + HALO, 128), F32),
            pltpu.VMEM((N_SUB, SUB_TILE + HALO, 128), F32),
            pltpu.VMEM((N_SUB, SUB_TILE + HALO, GROUP_W), F32),
            pltpu.VMEM((N_SUB, SUB_TILE + HALO, SSD_XBC), F32),
            pltpu.VMEM((SSD_GROUPS, 128, SSD_STATE), F32),
        ],
        compiler_params=pltpu.CompilerParams(
            dimension_semantics=("arbitrary",),
            vmem_limit_bytes=VMEM_LIMIT_BYTES),
        name="mixer",
    )(h2d, h2d, *params)


def _ffn_call(h2d, layer, g, wg, wu, wd, fg, final_norm):
    t, d = h2d.shape
    tile_spec = pl.BlockSpec((FFN_TILE, d), lambda i: (i, 0))
    row_spec = pl.BlockSpec((1, d), lambda i: (0, 0))
    return pl.pallas_call(
        functools.partial(_ffn_kernel, final_norm=final_norm),
        grid=(t // FFN_TILE,),
        in_specs=[tile_spec, _layer_spec(g, layer), _layer_spec(wg, layer),
                  _layer_spec(wu, layer), _layer_spec(wd, layer), row_spec],
        out_specs=tile_spec,
        out_shape=jax.ShapeDtypeStruct(h2d.shape, F32),
        scratch_shapes=[pltpu.VMEM((FFN_TILE, d), BF16), pltpu.VMEM((FFN_TILE, d), F32)],
        compiler_params=pltpu.CompilerParams(
            dimension_semantics=("parallel",),
            vmem_limit_bytes=VMEM_LIMIT_BYTES),
        name="ffn",
    )(h2d, g, wg, wu, wd, fg)


def _pack_layer_vectors(norm_mix_g, pool_b, pool_scale, sgu_ln_g, sgu_ln_b, ssd_dt_bias, ssd_a_log, ssd_d,
                        ssd_norm_g, ssd_conv_b, sconv_w, ssd_conv_w):
    depth = norm_mix_g.shape[0]

    def rep(v):
        return jnp.repeat(v, GROUP_W // SSD_HEADS, axis=-1)

    def pad_to_row(v):
        return jnp.pad(v, [(0, 0)] * (v.ndim - 1) + [(0, D_MODEL - v.shape[-1])])

    rows = [
        norm_mix_g,
        jnp.concatenate([pool_b, pool_scale, sgu_ln_g, sgu_ln_b], axis=-1),
        jnp.concatenate([rep(ssd_dt_bias), rep(ssd_a_log), rep(ssd_d), ssd_norm_g], axis=-1),
        pad_to_row(ssd_conv_b),
        pad_to_row(sconv_w.reshape(depth, SCONV_WIDTH * GROUP_W)),
    ]
    table = jnp.concatenate([r[:, None, :] for r in rows] + [pad_to_row(ssd_conv_w)], axis=1)
    assert table.shape == (depth, N_VEC_ROWS, D_MODEL)
    return table.astype(F32)


def kernel(x, norm_mix_g, w_in, pool_w, pool_b, pool_scale, sgu_ln_g, sgu_ln_b, sgu_w, sgu_b, sconv_w, ssd_conv_w, ssd_conv_b, ssd_dt_bias, ssd_a_log, ssd_d, ssd_norm_g, w_out, norm_ffn_g, w_gate, w_up, w_down, final_norm_g):
    b, s, d = x.shape
    depth = w_in.shape[0]
    vecs = _pack_layer_vectors(norm_mix_g, pool_b, pool_scale, sgu_ln_g, sgu_ln_b, ssd_dt_bias, ssd_a_log,
                               ssd_d, ssd_norm_g, ssd_conv_b, sconv_w, ssd_conv_w)
    win = w_in.astype(BF16)
    wdt = jnp.repeat(w_in[:, :, C_DT:], GROUP_W // SSD_HEADS, axis=2).astype(BF16)
    poolw = jnp.einsum('lgcd,gh->lgchd', pool_w, jnp.eye(POOL_GROUPS, dtype=F32)).reshape(
        depth, GROUP_W, GROUP_W).astype(BF16)
    sgub = jnp.repeat(jnp.swapaxes(sgu_b, 1, 2), GROUP_W // SGU_HEADS, axis=2)
    wout, wg, wu, wd = (w.astype(BF16) for w in (w_out, w_gate, w_up, w_down))
    fg = final_norm_g.reshape(1, d).astype(F32)
    mixer_params = (vecs, win, wdt, poolw, sgu_w, sgub, wout)

    h = x.reshape(b * s, d)
    for layer in range(depth):
        h = _mixer_call(h, s, layer, mixer_params)
        h = _ffn_call(h, layer, norm_ffn_g.reshape(depth, 1, d), wg, wu, wd, fg, final_norm=(layer == depth - 1))
    return h.reshape(b, s, d)
```

```python
import functools

import jax
import jax.numpy as jnp
from jax import lax
from jax.experimental import pallas as pl
from jax.experimental.pallas import tpu as pltpu

F32 = jnp.float32
BF16 = jnp.bfloat16

D_MODEL = 1024
GROUP_W = 256
POOL_GROUPS = 4
SGU_BLOCK = 128
SGU_HEADS = 4
HALO = 8
SCONV_WIDTH = 3
SSD_CONV = 4
SSD_CHUNK = 64
SSD_STATE = 128
SSD_GROUPS = 2
SSD_HEADS = 4
SSD_XBC = 768
D_FF = 2816
EPS = 1e-6

C_POOL, C_U, C_V, C_CB, C_CC, C_CH, C_Z, C_XBC, C_DT = 0, 256, 512, 768, 1024, 1280, 1536, 1792, 2560
PROJ_COLS = C_DT + GROUP_W

V_NORM_G, V_POOL_SGU, V_SSD, V_CONV_B, V_SCONV_W, V_CONV_W = 0, 1, 2, 3, 4, 5
N_VEC_ROWS = V_CONV_W + SSD_CONV

SEQ_TILE = 1024
SUB_TILE = 512
N_SUB = SEQ_TILE // SUB_TILE
assert N_SUB == 2
FFN_TILE = 1024
FFN_SUB = 512
assert FFN_TILE == 2 * FFN_SUB
FFN_CHUNK = 256
VMEM_LIMIT_BYTES = 56 * 1024 * 1024
PROJ_PIECES_AFTER_MIX_PIECE = (2, 1, 1, 1, 1, 1, 1, 1, 1, 1, 1, 0)


def _dot(a, b):
    return jnp.dot(a, b, preferred_element_type=F32)


NEG_LOG2_E = -1.4426950408889634


def _silu(x):
    return x * (1.0 / (1.0 + jnp.exp2(x * NEG_LOG2_E)))


def _rms_scale(x):
    return x * lax.rsqrt(jnp.mean(x * x, axis=-1, keepdims=True) + EPS)


def _proj_stage(h_ref, row0, vec_ref, win_ref, wdt_ref, hn_buf, proj_buf):
    g = vec_ref[V_NORM_G:V_NORM_G + 1, :]
    hn_buf[...] = (_rms_scale(h_ref[row0:row0 + SUB_TILE, :]) * g).astype(BF16)
    yield
    for c0 in range(0, C_DT, GROUP_W):
        proj_buf[:, c0:c0 + GROUP_W] = _dot(hn_buf[...], win_ref[:, c0:c0 + GROUP_W])
        yield
    proj_buf[:, C_DT:C_DT + GROUP_W] = _dot(hn_buf[...], wdt_ref[...])
    yield


def _mix_stage(k, pos0, h_ref, row0, out_ref, proj, prm, carry):
    vec_ref, poolw_ref, sguw_ref, sgub_ref, wout_ref = prm
    x_buf, s2_buf, s4_buf, s8_buf, sc_buf, xbc_buf, state_ref = carry
    ts = SUB_TILE
    nxt = (k + 1) % N_SUB
    rows = slice(row0, row0 + ts)
    n_chunks = ts // SSD_CHUNK

    def vec(row, slot, width=GROUP_W):
        return vec_ref[row:row + 1, slot * GROUP_W:slot * GROUP_W + width]

    def out_dot(m, y):
        return _dot(y.astype(BF16), wout_ref[m * GROUP_W:(m + 1) * GROUP_W, :])

    def with_halo(buf, val):
        buf[k, HALO:HALO + ts, :] = val
        buf[nxt, 0:HALO, :] = val[ts - HALO:ts, :]

    lane = lax.broadcasted_iota(jnp.int32, (1, 128), 1)
    lo_half = lane < 64

    xbc_buf[nxt, 0:HALO, :] = proj[ts - HALO:ts, C_XBC:C_XBC + SSD_XBC]
    xbc = []
    for part in range(SSD_XBC // GROUP_W):
        cols = slice(part * GROUP_W, (part + 1) * GROUP_W)
        pre = proj[:, C_XBC + part * GROUP_W:C_XBC + (part + 1) * GROUP_W]
        xbc_buf[k, HALO:HALO + ts, cols] = pre
        conv = pre * vec_ref[V_CONV_W + SSD_CONV - 1:V_CONV_W + SSD_CONV, cols] + vec_ref[V_CONV_B:V_CONV_B + 1, cols]
        for lag in range(1, SSD_CONV):
            tap = V_CONV_W + SSD_CONV - 1 - lag
            conv = conv + xbc_buf[k, HALO - lag:HALO - lag + ts, cols] * vec_ref[tap:tap + 1, cols]
        xbc.append(_silu(conv))
        yield
    xh, bmat, cmat = xbc

    dt = proj[:, C_DT:C_DT + GROUP_W] + vec(V_SSD, 0)
    delta = jnp.maximum(dt, 0.0) + jnp.log(1.0 + jnp.exp(-jnp.abs(dt)))
    da = delta * (-jnp.exp(vec(V_SSD, 1)))
    x_dt = xh * delta

    rr = lax.broadcasted_iota(jnp.int32, (SSD_CHUNK, 3 * SSD_CHUNK), 0)
    cc = lax.broadcasted_iota(jnp.int32, (SSD_CHUNK, 3 * SSD_CHUNK), 1) % SSD_CHUNK
    tri3 = jnp.where(cc <= rr, 1.0, 0.0).astype(BF16)
    da_hi = da.astype(BF16)
    rem = da - da_hi.astype(F32)
    da_mid = rem.astype(BF16)
    da_lo = (rem - da_mid.astype(F32)).astype(BF16)
    acs_chunks = []
    for c in range(n_chunks):
        crow = slice(c * SSD_CHUNK, (c + 1) * SSD_CHUNK)
        acs_chunks.append(_dot(tri3, jnp.concatenate([da_hi[crow], da_mid[crow], da_lo[crow]], axis=0)))
    yield

    v = proj[:, C_V:C_V + GROUP_W]
    mu = jnp.mean(v, axis=-1, keepdims=True)
    vc = v - mu
    var = jnp.mean(vc * vc, axis=-1, keepdims=True)
    vn = ((vc * lax.rsqrt(var + EPS)) * vec(V_POOL_SGU, 2) + vec(V_POOL_SGU, 3)).astype(BF16)
    ri = lax.broadcasted_iota(jnp.int32, (SGU_BLOCK, SGU_BLOCK), 0) // SSD_CHUNK
    ci = lax.broadcasted_iota(jnp.int32, (SGU_BLOCK, SGU_BLOCK), 1) // SSD_CHUNK
    chunk_causal = ri >= ci
    lane256 = lax.broadcasted_iota(jnp.int32, (1, GROUP_W), 1)
    w_heads = [jnp.where(chunk_causal, sguw_ref[hd], 0.0).astype(BF16) for hd in range(SGU_HEADS)]
    head_mix = [[_dot(w_heads[hd], vn[blk * SGU_BLOCK:(blk + 1) * SGU_BLOCK, :]) for hd in range(SGU_HEADS)]
                for blk in range(ts // SGU_BLOCK)]
    yield

    row64 = lax.broadcasted_iota(jnp.int32, (SSD_CHUNK, 128), 0)
    col64 = lax.broadcasted_iota(jnp.int32, (SSD_CHUNK, 128), 1) % SSD_CHUNK
    causal2 = col64 <= row64
    first_head_rows = lax.broadcasted_iota(jnp.int32, (128, 128), 0) < 64
    blocks = [(c, g) for c in range(n_chunks) for g in range(SSD_GROUPS)]

    def blk(arr, c, g):
        return arr[c * SSD_CHUNK:(c + 1) * SSD_CHUNK, g * 128:(g + 1) * 128]

    a2, b_bf, upd = {}, {}, {}
    for c, g in blocks:
        a2[c, g] = acs_chunks[c][:, g * 128:(g + 1) * 128]
        b_bf[c, g] = blk(bmat, c, g).astype(BF16)
        last = a2[c, g][SSD_CHUNK - 1:SSD_CHUNK, :]
        upd[c, g] = lax.dot_general((blk(x_dt, c, g) * jnp.exp(last - a2[c, g])).astype(BF16), b_bf[c, g],
                                    (((0,), (0,)), ((), ())), preferred_element_type=F32)
    yield

    xp = proj[:, C_POOL:C_POOL + GROUP_W]
    with_halo(x_buf, xp)
    s2 = xp + x_buf[k, HALO - 1:HALO - 1 + ts, :]
    with_halo(s2_buf, s2)
    s4 = s2 + s2_buf[k, HALO - 2:HALO - 2 + ts, :]
    with_halo(s4_buf, s4[:, 128:256])
    s8 = s4[:, 128:256] + s4_buf[k, HALO - 4:HALO - 4 + ts, :]
    with_halo(s8_buf, s8)
    s16 = s8 + s8_buf[k, 0:ts, :]
    pos1 = (lax.broadcasted_iota(jnp.int32, (ts, 128), 0) + (pos0 + 1)).astype(F32)
    cnt_a = jnp.where(lo_half, jnp.minimum(pos1, 2.0), jnp.minimum(pos1, 4.0))
    cnt_b = jnp.where(lo_half, jnp.minimum(pos1, 8.0), jnp.minimum(pos1, 16.0))
    pooled = jnp.concatenate([jnp.where(lo_half, s2[:, 0:128], s4[:, 0:128]) / cnt_a,
                              jnp.where(lo_half, s8, s16) / cnt_b], axis=1) - xp
    pool_mm = _dot(pooled.astype(BF16), poolw_ref[...])

    mixed_blocks = []
    for per_head in head_mix:
        mixed = per_head[0]
        for hd in range(1, SGU_HEADS):
            mixed = jnp.where(lane256 >= hd * 64, per_head[hd], mixed)
        mixed_blocks.append(mixed + sgub_ref[...])
    y_b = proj[:, C_U:C_U + GROUP_W] * jnp.concatenate(mixed_blocks, axis=0)
    d_b = out_dot(1, y_b)
    yield

    states = [state_ref[g] for g in range(SSD_GROUPS)]
    both = {}
    for c, g in blocks:
        rhs = jnp.concatenate([b_bf[c, g], b_bf[c, g], states[g].astype(BF16)], axis=0)
        both[c, g] = lax.dot_general(blk(cmat, c, g).astype(BF16), rhs, (((1,), (1,)), ((), ())),
                                     preferred_element_type=F32)
        last = a2[c, g][SSD_CHUNK - 1:SSD_CHUNK, :]
        chunk_decay = jnp.exp(jnp.where(first_head_rows, last[:, 0:1], last[:, 127:128]))
        states[g] = states[g] * chunk_decay + upd[c, g]
    for g in range(SSD_GROUPS):
        state_ref[g] = states[g]
    yield

    prod = proj[:, C_CC:C_CC + GROUP_W] * proj[:, C_CH:C_CH + GROUP_W]
    with_halo(sc_buf, prod)
    conv = prod * vec(V_SCONV_W, SCONV_WIDTH - 1)
    for lag in range(1, SCONV_WIDTH):
        conv = conv + sc_buf[k, HALO - lag:HALO - lag + ts, :] * vec(V_SCONV_W, SCONV_WIDTH - 1 - lag)
    y_c = proj[:, C_CB:C_CB + GROUP_W] * conv
    d_c = out_dot(2, y_c)
    y_a = (pool_mm + vec(V_POOL_SGU, 0)) * vec(V_POOL_SGU, 1)
    d_a = out_dot(0, y_a)
    out_ref[rows, :] = h_ref[rows, :] + d_b
    yield

    y_diag = {}
    for c, g in blocks:
        a_key = jnp.sum(jnp.where(row64 <= col64, blk(da, c, g), 0.0), axis=0, keepdims=True)
        decay = jnp.exp(jnp.where(causal2, a2[c, g] - a_key, -jnp.inf))
        x2 = blk(x_dt, c, g)
        x_bd = jnp.concatenate([jnp.where(lo_half, x2, 0.0), jnp.where(lo_half, 0.0, x2)], axis=0)
        y_diag[c, g] = _dot((both[c, g][:, 0:128] * decay).astype(BF16), x_bd.astype(BF16))
    out_ref[rows, :] += d_a + d_c
    yield

    y_ssd = jnp.concatenate(
        [jnp.concatenate([y_diag[c, g] + both[c, g][:, 128:256] * jnp.exp(a2[c, g]) for g in range(SSD_GROUPS)], axis=1)
         for c in range(n_chunks)], axis=0)
    y = (y_ssd + xh * vec(V_SSD, 2)) * _silu(proj[:, C_Z:C_Z + GROUP_W])
    y_d = jnp.concatenate([_rms_scale(y[:, 0:128]), _rms_scale(y[:, 128:256])], axis=1) * vec(V_SSD, 3)
    d_d = out_dot(3, y_d)
    yield
    out_ref[rows, :] += d_d
    yield


def _interleave(main, side):
    for n_side in PROJ_PIECES_AFTER_MIX_PIECE:
        next(main)
        for _ in range(n_side):
            next(side)
    assert next(main, "done") == "done" and next(side, "done") == "done"


def _mixer_kernel(h_ref, hnext_ref, vec_ref, win_ref, wdt_ref, poolw_ref, sguw_ref, sgub_ref, wout_ref,
                  out_ref, hn_buf, proj_a, proj_b, x_buf, s2_buf, s4_buf, s8_buf, sc_buf, xbc_buf,
                  state_ref, *, tiles_per_seq):
    prm = (vec_ref, poolw_ref, sguw_ref, sgub_ref, wout_ref)
    halo_bufs = (x_buf, s2_buf, s4_buf, s8_buf, sc_buf, xbc_buf)
    carry = halo_bufs + (state_ref,)
    step = pl.program_id(0)
    tile_in_seq = step % tiles_per_seq

    def proj_stage(src_ref, row0, dst):
        return _proj_stage(src_ref, row0, vec_ref, win_ref, wdt_ref, hn_buf, dst)

    @pl.when(step == 0)
    def _():
        for _ in proj_stage(h_ref, 0, proj_a):
            pass

    @pl.when(tile_in_seq == 0)
    def _():
        for buf in halo_bufs:
            buf[0, 0:HALO, :] = jnp.zeros((HALO, buf.shape[2]), F32)
        state_ref[...] = jnp.zeros(state_ref.shape, F32)

    pos0 = tile_in_seq * SEQ_TILE
    _interleave(_mix_stage(0, pos0, h_ref, 0, out_ref, proj_a, prm, carry),
                proj_stage(h_ref, SUB_TILE, proj_b))
    _interleave(_mix_stage(1, pos0 + SUB_TILE, h_ref, SUB_TILE, out_ref, proj_b, prm, carry),
                proj_stage(hnext_ref, 0, proj_a))


def _ffn_kernel(h_ref, hnext_ref, g_ref, wg_ref, wu_ref, wd_ref, fg_ref, out_ref,
                hn_a, hn_b, acc_a, acc_b, *, final_norm):
    def norm_into(src_ref, row0, hn_buf):
        hn_buf[...] = (_rms_scale(src_ref[row0:row0 + FFN_SUB, :]) * g_ref[...]).astype(BF16)

    def sub_tile(row0, hn_buf, acc_ref, norm_next):
        rows = slice(row0, row0 + FFN_SUB)
        acc_ref[...] = h_ref[rows, :]
        for j in range(D_FF // FFN_CHUNK):
            c0 = j * FFN_CHUNK
            hn = hn_buf[...]
            gate = _dot(hn, wg_ref[:, c0:c0 + FFN_CHUNK])
            up = _dot(hn, wu_ref[:, c0:c0 + FFN_CHUNK])
            act = (_silu(gate) * up).astype(BF16)
            acc_ref[...] += _dot(act, wd_ref[c0:c0 + FFN_CHUNK, :])
            if j == 0:
                norm_next()
        res = acc_ref[...]
        if final_norm:
            res = _rms_scale(res) * fg_ref[...]
        out_ref[rows, :] = res

    @pl.when(pl.program_id(0) == 0)
    def _():
        norm_into(h_ref, 0, hn_a)

    sub_tile(0, hn_a, acc_a, lambda: norm_into(h_ref, FFN_SUB, hn_b))
    sub_tile(FFN_SUB, hn_b, acc_b, lambda: norm_into(hnext_ref, 0, hn_a))


def _layer_spec(arr, layer):
    zeros = (0,) * (arr.ndim - 1)
    return pl.BlockSpec((None,) + arr.shape[1:], lambda i: (layer,) + zeros)


def _mixer_call(h2d, seq_len, layer, params):
    t, d = h2d.shape
    n_tiles = t // SEQ_TILE
    assert seq_len % SEQ_TILE == 0 and t % seq_len == 0
    tile_spec = pl.BlockSpec((SEQ_TILE, d), lambda i: (i, 0))
    next_spec = pl.BlockSpec((SUB_TILE, d), lambda i: (jnp.minimum(i + 1, n_tiles - 1) * N_SUB, 0))
    return pl.pallas_call(
        functools.partial(_mixer_kernel, tiles_per_seq=seq_len // SEQ_TILE),
        grid=(n_tiles,),
        in_specs=[tile_spec, next_spec] + [_layer_spec(p, layer) for p in params],
        out_specs=tile_spec,
        out_shape=jax.ShapeDtypeStruct(h2d.shape, F32),
        scratch_shapes=[
            pltpu.VMEM((SUB_TILE, d), BF16),
            pltpu.VMEM((SUB_TILE, PROJ_COLS), F32),
            pltpu.VMEM((SUB_TILE, PROJ_COLS), F32),
            pltpu.VMEM((N_SUB, SUB_TILE + HALO, GROUP_W), F32),
            pltpu.VMEM((N_SUB, SUB_TILE + HALO, GROUP_W), F32),
            pltpu.VMEM((N_SUB, SUB_TILE + HALO, 128), F32),
            pltpu.VMEM((N_SUB, SUB_TILE + HALO, 128), F32),
            pltpu.VMEM((N_SUB, SUB_TILE + HALO, GROUP_W), F32),
            pltpu.VMEM((N_SUB, SUB_TILE + HALO, SSD_XBC), F32),
            pltpu.VMEM((SSD_GROUPS, 128, SSD_STATE), F32),
        ],
        compiler_params=pltpu.CompilerParams(
            dimension_semantics=("arbitrary",),
            vmem_limit_bytes=VMEM_LIMIT_BYTES),
        name="mixer",
    )(h2d, h2d, *params)


def _ffn_call(h2d, layer, g, wg, wu, wd, fg, final_norm):
    t, d = h2d.shape
    n_tiles = t // FFN_TILE
    tile_spec = pl.BlockSpec((FFN_TILE, d), lambda i: (i, 0))
    next_spec = pl.BlockSpec((FFN_SUB, d), lambda i: (jnp.minimum(i + 1, n_tiles - 1) * (FFN_TILE // FFN_SUB), 0))
    row_spec = pl.BlockSpec((1, d), lambda i: (0, 0))
    return pl.pallas_call(
        functools.partial(_ffn_kernel, final_norm=final_norm),
        grid=(n_tiles,),
        in_specs=[tile_spec, next_spec, _layer_spec(g, layer), _layer_spec(wg, layer),
                  _layer_spec(wu, layer), _layer_spec(wd, layer), row_spec],
        out_specs=tile_spec,
        out_shape=jax.ShapeDtypeStruct(h2d.shape, F32),
        scratch_shapes=[pltpu.VMEM((FFN_SUB, d), BF16), pltpu.VMEM((FFN_SUB, d), BF16),
                        pltpu.VMEM((FFN_SUB, d), F32), pltpu.VMEM((FFN_SUB, d), F32)],
        compiler_params=pltpu.CompilerParams(
            dimension_semantics=("arbitrary",),
            vmem_limit_bytes=VMEM_LIMIT_BYTES),
        name="ffn",
    )(h2d, h2d, g, wg, wu, wd, fg)


def _pack_layer_vectors(norm_mix_g, pool_b, pool_scale, sgu_ln_g, sgu_ln_b, ssd_dt_bias, ssd_a_log, ssd_d,
                        ssd_norm_g, ssd_conv_b, sconv_w, ssd_conv_w):
    depth = norm_mix_g.shape[0]

    def rep(v):
        return jnp.repeat(v, GROUP_W // SSD_HEADS, axis=-1)

    def pad_to_row(v):
        return jnp.pad(v, [(0, 0)] * (v.ndim - 1) + [(0, D_MODEL - v.shape[-1])])

    rows = [
        norm_mix_g,
        jnp.concatenate([pool_b, pool_scale, sgu_ln_g, sgu_ln_b], axis=-1),
        jnp.concatenate([rep(ssd_dt_bias), rep(ssd_a_log), rep(ssd_d), ssd_norm_g], axis=-1),
        pad_to_row(ssd_conv_b),
        pad_to_row(sconv_w.reshape(depth, SCONV_WIDTH * GROUP_W)),
    ]
    table = jnp.concatenate([r[:, None, :] for r in rows] + [pad_to_row(ssd_conv_w)], axis=1)
    assert table.shape == (depth, N_VEC_ROWS, D_MODEL)
    return table.astype(F32)


def kernel(x, norm_mix_g, w_in, pool_w, pool_b, pool_scale, sgu_ln_g, sgu_ln_b, sgu_w, sgu_b, sconv_w, ssd_conv_w, ssd_conv_b, ssd_dt_bias, ssd_a_log, ssd_d, ssd_norm_g, w_out, norm_ffn_g, w_gate, w_up, w_down, final_norm_g):
    b, s, d = x.shape
    depth = w_in.shape[0]
    vecs = _pack_layer_vectors(norm_mix_g, pool_b, pool_scale, sgu_ln_g, sgu_ln_b, ssd_dt_bias, ssd_a_log,
                               ssd_d, ssd_norm_g, ssd_conv_b, sconv_w, ssd_conv_w)
    win = w_in.astype(BF16)
    wdt = jnp.repeat(w_in[:, :, C_DT:], GROUP_W // SSD_HEADS, axis=2).astype(BF16)
    poolw = jnp.einsum('lgcd,gh->lgchd', pool_w, jnp.eye(POOL_GROUPS, dtype=F32)).reshape(
        depth, GROUP_W, GROUP_W).astype(BF16)
    sgub = jnp.repeat(jnp.swapaxes(sgu_b, 1, 2), GROUP_W // SGU_HEADS, axis=2)
    wout, wg, wu, wd = (w.astype(BF16) for w in (w_out, w_gate, w_up, w_down))
    fg = final_norm_g.reshape(1, d).astype(F32)
    mixer_params = (vecs, win, wdt, poolw, sgu_w, sgub, wout)

    h = x.reshape(b * s, d)
    for layer in range(depth):
        h = _mixer_call(h, s, layer, mixer_params)
        h = _ffn_call(h, layer, norm_ffn_g.reshape(depth, 1, d), wg, wu, wd, fg, final_norm=(layer == depth - 1))
    return h.reshape(b, s, d)
```

```python
import functools

import jax
import jax.numpy as jnp
from jax import lax
from jax.experimental import pallas as pl
from jax.experimental.pallas import tpu as pltpu

F32 = jnp.float32
BF16 = jnp.bfloat16

D_MODEL = 1024
GROUP_W = 256
POOL_GROUPS = 4
SGU_BLOCK = 128
SGU_HEADS = 4
HALO = 8
SCONV_WIDTH = 3
SSD_CONV = 4
SSD_CHUNK = 64
SSD_STATE = 128
SSD_GROUPS = 2
SSD_HEADS = 4
SSD_XBC = 768
D_FF = 2816
EPS = 1e-6

C_POOL, C_U, C_V, C_CB, C_CC, C_CH, C_Z, C_XBC, C_DT = 0, 256, 512, 768, 1024, 1280, 1536, 1792, 2560
PROJ_COLS = C_DT + GROUP_W

V_NORM_G, V_POOL_SGU, V_SSD, V_CONV_B, V_SCONV_W, V_CONV_W = 0, 1, 2, 3, 4, 5
N_VEC_ROWS = V_CONV_W + SSD_CONV

SEQ_TILE = 1024
SUB_TILE = 512
N_SUB = SEQ_TILE // SUB_TILE
assert N_SUB == 2
FFN_TILE = 512
FFN_CHUNK = 256
VMEM_LIMIT_BYTES = 56 * 1024 * 1024
PROJ_PIECES_AFTER_MIX_PIECE = (2, 1, 1, 1, 1, 1, 1, 1, 1, 1, 1, 0)


def _dot(a, b):
    return jnp.dot(a, b, preferred_element_type=F32)


NEG_LOG2_E = -1.4426950408889634


def _silu(x):
    return x * (1.0 / (1.0 + jnp.exp2(x * NEG_LOG2_E)))


def _rms_scale(x):
    return x * lax.rsqrt(jnp.mean(x * x, axis=-1, keepdims=True) + EPS)


def _proj_stage(h_ref, row0, vec_ref, win_ref, wdt_ref, hn_buf, proj_buf):
    g = vec_ref[V_NORM_G:V_NORM_G + 1, :]
    hn_buf[...] = (_rms_scale(h_ref[row0:row0 + SUB_TILE, :]) * g).astype(BF16)
    yield
    for c0 in range(0, C_DT, GROUP_W):
        proj_buf[:, c0:c0 + GROUP_W] = _dot(hn_buf[...], win_ref[:, c0:c0 + GROUP_W])
        yield
    proj_buf[:, C_DT:C_DT + GROUP_W] = _dot(hn_buf[...], wdt_ref[...])
    yield


def _mix_stage(k, pos0, h_ref, row0, out_ref, proj, prm, carry):
    vec_ref, poolw_ref, sguw_ref, sgub_ref, wout_ref = prm
    x_buf, s2_buf, s4_buf, s8_buf, sc_buf, xbc_buf, state_ref = carry
    ts = SUB_TILE
    nxt = (k + 1) % N_SUB
    rows = slice(row0, row0 + ts)
    n_chunks = ts // SSD_CHUNK

    def vec(row, slot, width=GROUP_W):
        return vec_ref[row:row + 1, slot * GROUP_W:slot * GROUP_W + width]

    def out_dot(m, y):
        return _dot(y.astype(BF16), wout_ref[m * GROUP_W:(m + 1) * GROUP_W, :])

    def with_halo(buf, val):
        buf[k, HALO:HALO + ts, :] = val
        buf[nxt, 0:HALO, :] = val[ts - HALO:ts, :]

    lane = lax.broadcasted_iota(jnp.int32, (1, 128), 1)
    lo_half = lane < 64

    xbc_buf[nxt, 0:HALO, :] = proj[ts - HALO:ts, C_XBC:C_XBC + SSD_XBC]
    xbc = []
    for part in range(SSD_XBC // GROUP_W):
        cols = slice(part * GROUP_W, (part + 1) * GROUP_W)
        pre = proj[:, C_XBC + part * GROUP_W:C_XBC + (part + 1) * GROUP_W]
        xbc_buf[k, HALO:HALO + ts, cols] = pre
        conv = pre * vec_ref[V_CONV_W + SSD_CONV - 1:V_CONV_W + SSD_CONV, cols] + vec_ref[V_CONV_B:V_CONV_B + 1, cols]
        for lag in range(1, SSD_CONV):
            tap = V_CONV_W + SSD_CONV - 1 - lag
            conv = conv + xbc_buf[k, HALO - lag:HALO - lag + ts, cols] * vec_ref[tap:tap + 1, cols]
        xbc.append(_silu(conv))
        yield
    xh, bmat, cmat = xbc

    dt = proj[:, C_DT:C_DT + GROUP_W] + vec(V_SSD, 0)
    delta = jnp.maximum(dt, 0.0) + jnp.log(1.0 + jnp.exp(-jnp.abs(dt)))
    da = delta * (-jnp.exp(vec(V_SSD, 1)))
    x_dt = xh * delta

    rr = lax.broadcasted_iota(jnp.int32, (SSD_CHUNK, 3 * SSD_CHUNK), 0)
    cc = lax.broadcasted_iota(jnp.int32, (SSD_CHUNK, 3 * SSD_CHUNK), 1) % SSD_CHUNK
    tri3 = jnp.where(cc <= rr, 1.0, 0.0).astype(BF16)
    da_hi = da.astype(BF16)
    rem = da - da_hi.astype(F32)
    da_mid = rem.astype(BF16)
    da_lo = (rem - da_mid.astype(F32)).astype(BF16)
    acs_chunks = []
    for c in range(n_chunks):
        crow = slice(c * SSD_CHUNK, (c + 1) * SSD_CHUNK)
        acs_chunks.append(_dot(tri3, jnp.concatenate([da_hi[crow], da_mid[crow], da_lo[crow]], axis=0)))
    yield

    v = proj[:, C_V:C_V + GROUP_W]
    mu = jnp.mean(v, axis=-1, keepdims=True)
    vc = v - mu
    var = jnp.mean(vc * vc, axis=-1, keepdims=True)
    vn = ((vc * lax.rsqrt(var + EPS)) * vec(V_POOL_SGU, 2) + vec(V_POOL_SGU, 3)).astype(BF16)
    ri = lax.broadcasted_iota(jnp.int32, (SGU_BLOCK, SGU_BLOCK), 0) // SSD_CHUNK
    ci = lax.broadcasted_iota(jnp.int32, (SGU_BLOCK, SGU_BLOCK), 1) // SSD_CHUNK
    chunk_causal = ri >= ci
    lane256 = lax.broadcasted_iota(jnp.int32, (1, GROUP_W), 1)
    w_heads = [jnp.where(chunk_causal, sguw_ref[hd], 0.0).astype(BF16) for hd in range(SGU_HEADS)]
    head_mix = [[_dot(w_heads[hd], vn[blk * SGU_BLOCK:(blk + 1) * SGU_BLOCK, :]) for hd in range(SGU_HEADS)]
                for blk in range(ts // SGU_BLOCK)]
    yield

    row64 = lax.broadcasted_iota(jnp.int32, (SSD_CHUNK, 128), 0)
    col64 = lax.broadcasted_iota(jnp.int32, (SSD_CHUNK, 128), 1) % SSD_CHUNK
    causal2 = col64 <= row64
    blocks = [(c, g) for c in range(n_chunks) for g in range(SSD_GROUPS)]

    def blk(arr, c, g):
        return arr[c * SSD_CHUNK:(c + 1) * SSD_CHUNK, g * 128:(g + 1) * 128]

    a2, bt_bf, upd = {}, {}, {}
    for c, g in blocks:
        a2[c, g] = acs_chunks[c][:, g * 128:(g + 1) * 128]
        bt_bf[c, g] = blk(bmat, c, g).T.astype(BF16)
        last = a2[c, g][SSD_CHUNK - 1:SSD_CHUNK, :]
        upd[c, g] = _dot(bt_bf[c, g], (blk(x_dt, c, g) * jnp.exp(last - a2[c, g])).astype(BF16))
    yield

    xp = proj[:, C_POOL:C_POOL + GROUP_W]
    with_halo(x_buf, xp)
    s2 = xp + x_buf[k, HALO - 1:HALO - 1 + ts, :]
    with_halo(s2_buf, s2)
    s4 = s2 + s2_buf[k, HALO - 2:HALO - 2 + ts, :]
    with_halo(s4_buf, s4[:, 128:256])
    s8 = s4[:, 128:256] + s4_buf[k, HALO - 4:HALO - 4 + ts, :]
    with_halo(s8_buf, s8)
    s16 = s8 + s8_buf[k, 0:ts, :]
    pos1 = (lax.broadcasted_iota(jnp.int32, (ts, 128), 0) + (pos0 + 1)).astype(F32)
    cnt_a = jnp.where(lo_half, jnp.minimum(pos1, 2.0), jnp.minimum(pos1, 4.0))
    cnt_b = jnp.where(lo_half, jnp.minimum(pos1, 8.0), jnp.minimum(pos1, 16.0))
    pooled = jnp.concatenate([jnp.where(lo_half, s2[:, 0:128], s4[:, 0:128]) / cnt_a,
                              jnp.where(lo_half, s8, s16) / cnt_b], axis=1) - xp
    pool_mm = _dot(pooled.astype(BF16), poolw_ref[...])

    mixed_blocks = []
    for per_head in head_mix:
        mixed = per_head[0]
        for hd in range(1, SGU_HEADS):
            mixed = jnp.where(lane256 >= hd * 64, per_head[hd], mixed)
        mixed_blocks.append(mixed + sgub_ref[...])
    y_b = proj[:, C_U:C_U + GROUP_W] * jnp.concatenate(mixed_blocks, axis=0)
    d_b = out_dot(1, y_b)
    yield

    states = [state_ref[g] for g in range(SSD_GROUPS)]
    both = {}
    for c, g in blocks:
        rhs = jnp.concatenate([bt_bf[c, g], bt_bf[c, g], states[g].astype(BF16)], axis=1)
        both[c, g] = _dot(blk(cmat, c, g).astype(BF16), rhs)
        last = a2[c, g][SSD_CHUNK - 1:SSD_CHUNK, :]
        states[g] = states[g] * jnp.exp(last) + upd[c, g]
    for g in range(SSD_GROUPS):
        state_ref[g] = states[g]
    yield

    prod = proj[:, C_CC:C_CC + GROUP_W] * proj[:, C_CH:C_CH + GROUP_W]
    with_halo(sc_buf, prod)
    conv = prod * vec(V_SCONV_W, SCONV_WIDTH - 1)
    for lag in range(1, SCONV_WIDTH):
        conv = conv + sc_buf[k, HALO - lag:HALO - lag + ts, :] * vec(V_SCONV_W, SCONV_WIDTH - 1 - lag)
    y_c = proj[:, C_CB:C_CB + GROUP_W] * conv
    d_c = out_dot(2, y_c)
    y_a = (pool_mm + vec(V_POOL_SGU, 0)) * vec(V_POOL_SGU, 1)
    d_a = out_dot(0, y_a)
    out_ref[rows, :] = h_ref[rows, :] + d_b
    yield

    y_diag = {}
    for c, g in blocks:
        a_key = jnp.sum(jnp.where(row64 <= col64, blk(da, c, g), 0.0), axis=0, keepdims=True)
        decay = jnp.exp(jnp.where(causal2, a2[c, g] - a_key, -jnp.inf))
        x2 = blk(x_dt, c, g)
        x_bd = jnp.concatenate([jnp.where(lo_half, x2, 0.0), jnp.where(lo_half, 0.0, x2)], axis=0)
        y_diag[c, g] = _dot((both[c, g][:, 0:128] * decay).astype(BF16), x_bd.astype(BF16))
    out_ref[rows, :] += d_a + d_c
    yield

    y_ssd = jnp.concatenate(
        [jnp.concatenate([y_diag[c, g] + both[c, g][:, 128:256] * jnp.exp(a2[c, g]) for g in range(SSD_GROUPS)], axis=1)
         for c in range(n_chunks)], axis=0)
    y = (y_ssd + xh * vec(V_SSD, 2)) * _silu(proj[:, C_Z:C_Z + GROUP_W])
    y_d = jnp.concatenate([_rms_scale(y[:, 0:128]), _rms_scale(y[:, 128:256])], axis=1) * vec(V_SSD, 3)
    d_d = out_dot(3, y_d)
    yield
    out_ref[rows, :] += d_d
    yield


def _interleave(main, side):
    for n_side in PROJ_PIECES_AFTER_MIX_PIECE:
        next(main)
        for _ in range(n_side):
            next(side)
    assert next(main, "done") == "done" and next(side, "done") == "done"


def _mixer_kernel(h_ref, hnext_ref, vec_ref, win_ref, wdt_ref, poolw_ref, sguw_ref, sgub_ref, wout_ref,
                  out_ref, hn_buf, proj_a, proj_b, x_buf, s2_buf, s4_buf, s8_buf, sc_buf, xbc_buf,
                  state_ref, *, tiles_per_seq):
    prm = (vec_ref, poolw_ref, sguw_ref, sgub_ref, wout_ref)
    halo_bufs = (x_buf, s2_buf, s4_buf, s8_buf, sc_buf, xbc_buf)
    carry = halo_bufs + (state_ref,)
    step = pl.program_id(0)
    tile_in_seq = step % tiles_per_seq

    def proj_stage(src_ref, row0, dst):
        return _proj_stage(src_ref, row0, vec_ref, win_ref, wdt_ref, hn_buf, dst)

    @pl.when(step == 0)
    def _():
        for _ in proj_stage(h_ref, 0, proj_a):
            pass

    @pl.when(tile_in_seq == 0)
    def _():
        for buf in halo_bufs:
            buf[0, 0:HALO, :] = jnp.zeros((HALO, buf.shape[2]), F32)
        state_ref[...] = jnp.zeros(state_ref.shape, F32)

    pos0 = tile_in_seq * SEQ_TILE
    _interleave(_mix_stage(0, pos0, h_ref, 0, out_ref, proj_a, prm, carry),
                proj_stage(h_ref, SUB_TILE, proj_b))
    _interleave(_mix_stage(1, pos0 + SUB_TILE, h_ref, SUB_TILE, out_ref, proj_b, prm, carry),
                proj_stage(hnext_ref, 0, proj_a))


def _ffn_kernel(h_ref, g_ref, wg_ref, wu_ref, wd_ref, fg_ref, out_ref, hn_ref, acc_ref, *, final_norm):
    h = h_ref[...]
    hn_ref[...] = (_rms_scale(h) * g_ref[...]).astype(BF16)
    acc_ref[...] = h
    for j in range(D_FF // FFN_CHUNK):
        c0 = j * FFN_CHUNK
        hn = hn_ref[...]
        gate = _dot(hn, wg_ref[:, c0:c0 + FFN_CHUNK].astype(BF16))
        up = _dot(hn, wu_ref[:, c0:c0 + FFN_CHUNK].astype(BF16))
        act = (_silu(gate) * up).astype(BF16)
        acc_ref[...] += _dot(act, wd_ref[c0:c0 + FFN_CHUNK, :].astype(BF16))
    res = acc_ref[...]
    if final_norm:
        res = _rms_scale(res) * fg_ref[...]
    out_ref[...] = res


def _layer_spec(arr, layer):
    zeros = (0,) * (arr.ndim - 1)
    return pl.BlockSpec((None,) + arr.shape[1:], lambda i: (layer,) + zeros)


def _mixer_call(h2d, seq_len, layer, params):
    t, d = h2d.shape
    n_tiles = t // SEQ_TILE
    assert seq_len % SEQ_TILE == 0 and t % seq_len == 0
    tile_spec = pl.BlockSpec((SEQ_TILE, d), lambda i: (i, 0))
    next_spec = pl.BlockSpec((SUB_TILE, d), lambda i: (jnp.minimum(i + 1, n_tiles - 1) * N_SUB, 0))
    return pl.pallas_call(
        functools.partial(_mixer_kernel, tiles_per_seq=seq_len // SEQ_TILE),
        grid=(n_tiles,),
        in_specs=[tile_spec, next_spec] + [_layer_spec(p, layer) for p in params],
        out_specs=tile_spec,
        out_shape=jax.ShapeDtypeStruct(h2d.shape, F32),
        scratch_shapes=[
            pltpu.VMEM((SUB_TILE, d), BF16),
            pltpu.VMEM((SUB_TILE, PROJ_COLS), F32),
            pltpu.VMEM((SUB_TILE, PROJ_COLS), F32),
            pltpu.VMEM((N_SUB, SUB_TILE + HALO, GROUP_W), F32),
            pltpu.VMEM((N_SUB, SUB_TILE + HALO, GROUP_W), F32),
            pltpu.VMEM((N_SUB, SUB_TILE + HALO, 128), F32),
            pltpu.VMEM((N_SUB, SUB_TILE + HALO, 128), F32),
            pltpu.VMEM((N_SUB, SUB_TILE + HALO, GROUP_W), F32),
            pltpu.VMEM((N_SUB, SUB_TILE + HALO, SSD_XBC), F32),
            pltpu.VMEM((SSD_GROUPS, 128, SSD_STATE), F32),
        ],
        compiler_params=pltpu.CompilerParams(
            dimension_semantics=("arbitrary",),
            vmem_limit_bytes=VMEM_LIMIT_BYTES),
        name="mixer",
    )(h2d, h2d, *params)


def _ffn_call(h2d, layer, g, wg, wu, wd, fg, final_norm):
    t, d = h2d.shape
    tile_spec = pl.BlockSpec((FFN_TILE, d), lambda i: (i, 0))
    row_spec = pl.BlockSpec((1, d), lambda i: (0, 0))
    return pl.pallas_call(
        functools.partial(_ffn_kernel, final_norm=final_norm),
        grid=(t // FFN_TILE,),
        in_specs=[tile_spec, _layer_spec(g, layer), _layer_spec(wg, layer),
                  _layer_spec(wu, layer), _layer_spec(wd, layer), row_spec],
        out_specs=tile_spec,
        out_shape=jax.ShapeDtypeStruct(h2d.shape, F32),
        scratch_shapes=[pltpu.VMEM((FFN_TILE, d), BF16), pltpu.VMEM((FFN_TILE, d), F32)],
        compiler_params=pltpu.CompilerParams(
            dimension_semantics=("parallel",),
            vmem_limit_bytes=VMEM_LIMIT_BYTES),
        name="ffn",
    )(h2d, g, wg, wu, wd, fg)


def _pack_layer_vectors(norm_mix_g, pool_b, pool_scale, sgu_ln_g, sgu_ln_b, ssd_dt_bias, ssd_a_log, ssd_d,
                        ssd_norm_g, ssd_conv_b, sconv_w, ssd_conv_w):
    depth = norm_mix_g.shape[0]

    def rep(v):
        return jnp.repeat(v, GROUP_W // SSD_HEADS, axis=-1)

    def pad_to_row(v):
        return jnp.pad(v, [(0, 0)] * (v.ndim - 1) + [(0, D_MODEL - v.shape[-1])])

    rows = [
        norm_mix_g,
        jnp.concatenate([pool_b, pool_scale, sgu_ln_g, sgu_ln_b], axis=-1),
        jnp.concatenate([rep(ssd_dt_bias), rep(ssd_a_log), rep(ssd_d), ssd_norm_g], axis=-1),
        pad_to_row(ssd_conv_b),
        pad_to_row(sconv_w.reshape(depth, SCONV_WIDTH * GROUP_W)),
    ]
    table = jnp.concatenate([r[:, None, :] for r in rows] + [pad_to_row(ssd_conv_w)], axis=1)
    assert table.shape == (depth, N_VEC_ROWS, D_MODEL)
    return table.astype(F32)


def kernel(x, norm_mix_g, w_in, pool_w, pool_b, pool_scale, sgu_ln_g, sgu_ln_b, sgu_w, sgu_b, sconv_w, ssd_conv_w, ssd_conv_b, ssd_dt_bias, ssd_a_log, ssd_d, ssd_norm_g, w_out, norm_ffn_g, w_gate, w_up, w_down, final_norm_g):
    b, s, d = x.shape
    depth = w_in.shape[0]
    vecs = _pack_layer_vectors(norm_mix_g, pool_b, pool_scale, sgu_ln_g, sgu_ln_b, ssd_dt_bias, ssd_a_log,
                               ssd_d, ssd_norm_g, ssd_conv_b, sconv_w, ssd_conv_w)
    win = w_in.astype(BF16)
    wdt = jnp.repeat(w_in[:, :, C_DT:], GROUP_W // SSD_HEADS, axis=2).astype(BF16)
    poolw = jnp.einsum('lgcd,gh->lgchd', pool_w, jnp.eye(POOL_GROUPS, dtype=F32)).reshape(
        depth, GROUP_W, GROUP_W).astype(BF16)
    sgub = jnp.repeat(jnp.swapaxes(sgu_b, 1, 2), GROUP_W // SGU_HEADS, axis=2)
    wout = w_out.astype(BF16)
    fg = final_norm_g.reshape(1, d).astype(F32)
    mixer_params = (vecs, win, wdt, poolw, sgu_w, sgub, wout)

    h = x.reshape(b * s, d)
    for layer in range(depth):
        h = _mixer_call(h, s, layer, mixer_params)
        h = _ffn_call(h, layer, norm_ffn_g.reshape(depth, 1, d), w_gate, w_up, w_down, fg,
                      final_norm=(layer == depth - 1))
    return h.reshape(b, s, d)
```

```python
import functools

import jax
import jax.numpy as jnp
from jax import lax
from jax.experimental import pallas as pl
from jax.experimental.pallas import tpu as pltpu

F32 = jnp.float32
BF16 = jnp.bfloat16

D_MODEL = 1024
GROUP_W = 256
POOL_GROUPS = 4
SGU_BLOCK = 128
SGU_HEADS = 4
HALO = 8
SCONV_WIDTH = 3
SSD_CONV = 4
SSD_CHUNK = 64
SSD_STATE = 128
SSD_GROUPS = 2
SSD_HEADS = 4
SSD_XBC = 768
D_FF = 2816
EPS = 1e-6

C_POOL, C_U, C_V, C_CB, C_CC, C_CH, C_Z, C_XBC, C_DT = 0, 256, 512, 768, 1024, 1280, 1536, 1792, 2560
PROJ_COLS = C_DT + GROUP_W

V_NORM_G, V_POOL_SGU, V_SSD, V_CONV_B, V_SCONV_W, V_CONV_W = 0, 1, 2, 3, 4, 5
N_VEC_ROWS = V_CONV_W + SSD_CONV

SEQ_TILE = 1024
SUB_TILE = 512
N_SUB = SEQ_TILE // SUB_TILE
assert N_SUB == 2
FFN_TILE = 1024
FFN_CHUNK = 256
VMEM_LIMIT_BYTES = 56 * 1024 * 1024
PROJ_PIECES_AFTER_MIX_PIECE = (2, 1, 1, 1, 1, 1, 1, 1, 1, 1, 1, 0)


def _dot(a, b):
    return jnp.dot(a, b, preferred_element_type=F32)


NEG_LOG2_E = -1.4426950408889634


def _silu(x):
    return x * (1.0 / (1.0 + jnp.exp2(x * NEG_LOG2_E)))


def _rms_scale(x):
    return x * lax.rsqrt(jnp.mean(x * x, axis=-1, keepdims=True) + EPS)


def _proj_stage(h_ref, row0, vec_ref, win_ref, wdt_ref, hn_buf, proj_buf):
    g = vec_ref[V_NORM_G:V_NORM_G + 1, :]
    hn_buf[...] = (_rms_scale(h_ref[row0:row0 + SUB_TILE, :]) * g).astype(BF16)
    yield
    for c0 in range(0, C_DT, GROUP_W):
        proj_buf[:, c0:c0 + GROUP_W] = _dot(hn_buf[...], win_ref[:, c0:c0 + GROUP_W])
        yield
    proj_buf[:, C_DT:C_DT + GROUP_W] = _dot(hn_buf[...], wdt_ref[...])
    yield


def _mix_stage(k, pos0, h_ref, row0, out_ref, proj, prm, carry):
    vec_ref, poolw_ref, sguw_ref, sgub_ref, wout_ref = prm
    x_buf, s2_buf, s4_buf, s8_buf, sc_buf, xbc_buf, state_ref = carry
    ts = SUB_TILE
    nxt = (k + 1) % N_SUB
    rows = slice(row0, row0 + ts)
    n_chunks = ts // SSD_CHUNK

    def vec(row, slot, width=GROUP_W):
        return vec_ref[row:row + 1, slot * GROUP_W:slot * GROUP_W + width]

    def out_dot(m, y):
        return _dot(y.astype(BF16), wout_ref[m * GROUP_W:(m + 1) * GROUP_W, :])

    def with_halo(buf, val):
        buf[k, HALO:HALO + ts, :] = val
        buf[nxt, 0:HALO, :] = val[ts - HALO:ts, :]

    lane = lax.broadcasted_iota(jnp.int32, (1, 128), 1)
    lo_half = lane < 64

    xbc_buf[nxt, 0:HALO, :] = proj[ts - HALO:ts, C_XBC:C_XBC + SSD_XBC]
    xbc = []
    for part in range(SSD_XBC // GROUP_W):
        cols = slice(part * GROUP_W, (part + 1) * GROUP_W)
        pre = proj[:, C_XBC + part * GROUP_W:C_XBC + (part + 1) * GROUP_W]
        xbc_buf[k, HALO:HALO + ts, cols] = pre
        conv = pre * vec_ref[V_CONV_W + SSD_CONV - 1:V_CONV_W + SSD_CONV, cols] + vec_ref[V_CONV_B:V_CONV_B + 1, cols]
        for lag in range(1, SSD_CONV):
            tap = V_CONV_W + SSD_CONV - 1 - lag
            conv = conv + xbc_buf[k, HALO - lag:HALO - lag + ts, cols] * vec_ref[tap:tap + 1, cols]
        xbc.append(_silu(conv))
        yield
    xh, bmat, cmat = xbc

    dt = proj[:, C_DT:C_DT + GROUP_W] + vec(V_SSD, 0)
    delta = jnp.maximum(dt, 0.0) + jnp.log(1.0 + jnp.exp(-jnp.abs(dt)))
    da = delta * (-jnp.exp(vec(V_SSD, 1)))
    x_dt = xh * delta

    rr = lax.broadcasted_iota(jnp.int32, (SSD_CHUNK, 3 * SSD_CHUNK), 0)
    cc = lax.broadcasted_iota(jnp.int32, (SSD_CHUNK, 3 * SSD_CHUNK), 1) % SSD_CHUNK
    tri3 = jnp.where(cc <= rr, 1.0, 0.0).astype(BF16)
    da_hi = da.astype(BF16)
    rem = da - da_hi.astype(F32)
    da_mid = rem.astype(BF16)
    da_lo = (rem - da_mid.astype(F32)).astype(BF16)
    acs_chunks = []
    for c in range(n_chunks):
        crow = slice(c * SSD_CHUNK, (c + 1) * SSD_CHUNK)
        acs_chunks.append(_dot(tri3, jnp.concatenate([da_hi[crow], da_mid[crow], da_lo[crow]], axis=0)))
    yield

    v = proj[:, C_V:C_V + GROUP_W]
    mu = jnp.mean(v, axis=-1, keepdims=True)
    vc = v - mu
    var = jnp.mean(vc * vc, axis=-1, keepdims=True)
    vn = ((vc * lax.rsqrt(var + EPS)) * vec(V_POOL_SGU, 2) + vec(V_POOL_SGU, 3)).astype(BF16)
    ri = lax.broadcasted_iota(jnp.int32, (SGU_BLOCK, SGU_BLOCK), 0) // SSD_CHUNK
    ci = lax.broadcasted_iota(jnp.int32, (SGU_BLOCK, SGU_BLOCK), 1) // SSD_CHUNK
    chunk_causal = ri >= ci
    lane256 = lax.broadcasted_iota(jnp.int32, (1, GROUP_W), 1)
    w_heads = [jnp.where(chunk_causal, sguw_ref[hd], 0.0).astype(BF16) for hd in range(SGU_HEADS)]
    head_mix = [[_dot(w_heads[hd], vn[blk * SGU_BLOCK:(blk + 1) * SGU_BLOCK, :]) for hd in range(SGU_HEADS)]
                for blk in range(ts // SGU_BLOCK)]
    yield

    row64 = lax.broadcasted_iota(jnp.int32, (SSD_CHUNK, 128), 0)
    col64 = lax.broadcasted_iota(jnp.int32, (SSD_CHUNK, 128), 1) % SSD_CHUNK
    causal2 = col64 <= row64
    blocks = [(c, g) for c in range(n_chunks) for g in range(SSD_GROUPS)]

    def blk(arr, c, g):
        return arr[c * SSD_CHUNK:(c + 1) * SSD_CHUNK, g * 128:(g + 1) * 128]

    a2, bt_bf, upd = {}, {}, {}
    for c, g in blocks:
        a2[c, g] = acs_chunks[c][:, g * 128:(g + 1) * 128]
        bt_bf[c, g] = blk(bmat, c, g).T.astype(BF16)
        last = a2[c, g][SSD_CHUNK - 1:SSD_CHUNK, :]
        upd[c, g] = _dot(bt_bf[c, g], (blk(x_dt, c, g) * jnp.exp(last - a2[c, g])).astype(BF16))
    yield

    xp = proj[:, C_POOL:C_POOL + GROUP_W]
    with_halo(x_buf, xp)
    s2 = xp + x_buf[k, HALO - 1:HALO - 1 + ts, :]
    with_halo(s2_buf, s2)
    s4 = s2 + s2_buf[k, HALO - 2:HALO - 2 + ts, :]
    with_halo(s4_buf, s4[:, 128:256])
    s8 = s4[:, 128:256] + s4_buf[k, HALO - 4:HALO - 4 + ts, :]
    with_halo(s8_buf, s8)
    s16 = s8 + s8_buf[k, 0:ts, :]
    pos1 = (lax.broadcasted_iota(jnp.int32, (ts, 128), 0) + (pos0 + 1)).astype(F32)
    cnt_a = jnp.where(lo_half, jnp.minimum(pos1, 2.0), jnp.minimum(pos1, 4.0))
    cnt_b = jnp.where(lo_half, jnp.minimum(pos1, 8.0), jnp.minimum(pos1, 16.0))
    pooled = jnp.concatenate([jnp.where(lo_half, s2[:, 0:128], s4[:, 0:128]) / cnt_a,
                              jnp.where(lo_half, s8, s16) / cnt_b], axis=1) - xp
    pool_mm = _dot(pooled.astype(BF16), poolw_ref[...])

    mixed_blocks = []
    for per_head in head_mix:
        mixed = per_head[0]
        for hd in range(1, SGU_HEADS):
            mixed = jnp.where(lane256 >= hd * 64, per_head[hd], mixed)
        mixed_blocks.append(mixed + sgub_ref[...])
    y_b = proj[:, C_U:C_U + GROUP_W] * jnp.concatenate(mixed_blocks, axis=0)
    d_b = out_dot(1, y_b)
    yield

    states = [state_ref[g] for g in range(SSD_GROUPS)]
    both = {}
    for c, g in blocks:
        rhs = jnp.concatenate([bt_bf[c, g], bt_bf[c, g], states[g].astype(BF16)], axis=1)
        both[c, g] = _dot(blk(cmat, c, g).astype(BF16), rhs)
        last = a2[c, g][SSD_CHUNK - 1:SSD_CHUNK, :]
        states[g] = states[g] * jnp.exp(last) + upd[c, g]
    for g in range(SSD_GROUPS):
        state_ref[g] = states[g]
    yield

    prod = proj[:, C_CC:C_CC + GROUP_W] * proj[:, C_CH:C_CH + GROUP_W]
    with_halo(sc_buf, prod)
    conv = prod * vec(V_SCONV_W, SCONV_WIDTH - 1)
    for lag in range(1, SCONV_WIDTH):
        conv = conv + sc_buf[k, HALO - lag:HALO - lag + ts, :] * vec(V_SCONV_W, SCONV_WIDTH - 1 - lag)
    y_c = proj[:, C_CB:C_CB + GROUP_W] * conv
    d_c = out_dot(2, y_c)
    y_a = (pool_mm + vec(V_POOL_SGU, 0)) * vec(V_POOL_SGU, 1)
    d_a = out_dot(0, y_a)
    out_ref[rows, :] = h_ref[rows, :] + d_b
    yield

    y_diag = {}
    for c, g in blocks:
        a_key = jnp.sum(jnp.where(row64 <= col64, blk(da, c, g), 0.0), axis=0, keepdims=True)
        decay = jnp.exp(jnp.where(causal2, a2[c, g] - a_key, -jnp.inf))
        x2 = blk(x_dt, c, g)
        x_bd = jnp.concatenate([jnp.where(lo_half, x2, 0.0), jnp.where(lo_half, 0.0, x2)], axis=0)
        y_diag[c, g] = _dot((both[c, g][:, 0:128] * decay).astype(BF16), x_bd.astype(BF16))
    out_ref[rows, :] += d_a + d_c
    yield

    y_ssd = jnp.concatenate(
        [jnp.concatenate([y_diag[c, g] + both[c, g][:, 128:256] * jnp.exp(a2[c, g]) for g in range(SSD_GROUPS)], axis=1)
         for c in range(n_chunks)], axis=0)
    y = (y_ssd + xh * vec(V_SSD, 2)) * _silu(proj[:, C_Z:C_Z + GROUP_W])
    y_d = jnp.concatenate([_rms_scale(y[:, 0:128]), _rms_scale(y[:, 128:256])], axis=1) * vec(V_SSD, 3)
    d_d = out_dot(3, y_d)
    yield
    out_ref[rows, :] += d_d
    yield


def _interleave(main, side):
    for n_side in PROJ_PIECES_AFTER_MIX_PIECE:
        next(main)
        for _ in range(n_side):
            next(side)
    assert next(main, "done") == "done" and next(side, "done") == "done"


def _mixer_kernel(h_ref, hnext_ref, vec_ref, win_ref, wdt_ref, poolw_ref, sguw_ref, sgub_ref, wout_ref,
                  out_ref, hn_buf, proj_a, proj_b, x_buf, s2_buf, s4_buf, s8_buf, sc_buf, xbc_buf,
                  state_ref, *, tiles_per_seq):
    prm = (vec_ref, poolw_ref, sguw_ref, sgub_ref, wout_ref)
    halo_bufs = (x_buf, s2_buf, s4_buf, s8_buf, sc_buf, xbc_buf)
    carry = halo_bufs + (state_ref,)
    step = pl.program_id(0)
    tile_in_seq = step % tiles_per_seq

    def proj_stage(src_ref, row0, dst):
        return _proj_stage(src_ref, row0, vec_ref, win_ref, wdt_ref, hn_buf, dst)

    @pl.when(step == 0)
    def _():
        for _ in proj_stage(h_ref, 0, proj_a):
            pass

    @pl.when(tile_in_seq == 0)
    def _():
        for buf in halo_bufs:
            buf[0, 0:HALO, :] = jnp.zeros((HALO, buf.shape[2]), F32)
        state_ref[...] = jnp.zeros(state_ref.shape, F32)

    pos0 = tile_in_seq * SEQ_TILE
    _interleave(_mix_stage(0, pos0, h_ref, 0, out_ref, proj_a, prm, carry),
                proj_stage(h_ref, SUB_TILE, proj_b))
    _interleave(_mix_stage(1, pos0 + SUB_TILE, h_ref, SUB_TILE, out_ref, proj_b, prm, carry),
                proj_stage(hnext_ref, 0, proj_a))


def _ffn_kernel(h_ref, g_ref, wg_ref, wu_ref, wd_ref, fg_ref, out_ref, hn_ref, acc_ref, *, final_norm):
    h = h_ref[...]
    hn_ref[...] = (_rms_scale(h) * g_ref[...]).astype(BF16)
    acc_ref[...] = h
    for j in range(D_FF // FFN_CHUNK):
        c0 = j * FFN_CHUNK
        hn = hn_ref[...]
        gate = _dot(hn, wg_ref[:, c0:c0 + FFN_CHUNK].astype(BF16))
        up = _dot(hn, wu_ref[:, c0:c0 + FFN_CHUNK].astype(BF16))
        act = (_silu(gate) * up).astype(BF16)
        acc_ref[...] += _dot(act, wd_ref[c0:c0 + FFN_CHUNK, :])
    res = acc_ref[...]
    if final_norm:
        res = _rms_scale(res) * fg_ref[...]
    out_ref[...] = res


def _layer_spec(arr, layer):
    zeros = (0,) * (arr.ndim - 1)
    return pl.BlockSpec((None,) + arr.shape[1:], lambda i: (layer,) + zeros)


def _mixer_call(h2d, seq_len, layer, params):
    t, d = h2d.shape
    n_tiles = t // SEQ_TILE
    assert seq_len % SEQ_TILE == 0 and t % seq_len == 0
    tile_spec = pl.BlockSpec((SEQ_TILE, d), lambda i: (i, 0))
    next_spec = pl.BlockSpec((SUB_TILE, d), lambda i: (jnp.minimum(i + 1, n_tiles - 1) * N_SUB, 0))
    return pl.pallas_call(
        functools.partial(_mixer_kernel, tiles_per_seq=seq_len // SEQ_TILE),
        grid=(n_tiles,),
        in_specs=[tile_spec, next_spec] + [_layer_spec(p, layer) for p in params],
        out_specs=tile_spec,
        out_shape=jax.ShapeDtypeStruct(h2d.shape, F32),
        scratch_shapes=[
            pltpu.VMEM((SUB_TILE, d), BF16),
            pltpu.VMEM((SUB_TILE, PROJ_COLS), F32),
            pltpu.VMEM((SUB_TILE, PROJ_COLS), F32),
            pltpu.VMEM((N_SUB, SUB_TILE + HALO, GROUP_W), F32),
            pltpu.VMEM((N_SUB, SUB_TILE + HALO, GROUP_W), F32),
            pltpu.VMEM((N_SUB, SUB_TILE + HALO, 128), F32),
            pltpu.VMEM((N_SUB, SUB_TILE + HALO, 128), F32),
            pltpu.VMEM((N_SUB, SUB_TILE + HALO, GROUP_W), F32),
            pltpu.VMEM((N_SUB, SUB_TILE + HALO, SSD_XBC), F32),
            pltpu.VMEM((SSD_GROUPS, 128, SSD_STATE), F32),
        ],
        compiler_params=pltpu.CompilerParams(
            dimension_semantics=("arbitrary",),
            vmem_limit_bytes=VMEM_LIMIT_BYTES),
        name="mixer",
    )(h2d, h2d, *params)


def _ffn_call(h2d, layer, g, wg, wu, wd, fg, final_norm):
    t, d = h2d.shape
    tile_spec = pl.BlockSpec((FFN_TILE, d), lambda i: (i, 0))
    row_spec = pl.BlockSpec((1, d), lambda i: (0, 0))
    return pl.pallas_call(
        functools.partial(_ffn_kernel, final_norm=final_norm),
        grid=(t // FFN_TILE,),
        in_specs=[tile_spec, _layer_spec(g, layer), _layer_spec(wg, layer),
                  _layer_spec(wu, layer), _layer_spec(wd, layer), row_spec],
        out_specs=tile_spec,
        out_shape=jax.ShapeDtypeStruct(h2d.shape, F32),
        scratch_shapes=[pltpu.VMEM((FFN_TILE, d), BF16), pltpu.VMEM((FFN_TILE, d), F32)],
        compiler_params=pltpu.CompilerParams(
            dimension_semantics=("parallel",),
            vmem_limit_bytes=VMEM_LIMIT_BYTES),
        name="ffn",
    )(h2d, g, wg, wu, wd, fg)


def _pack_layer_vectors(norm_mix_g, pool_b, pool_scale, sgu_ln_g, sgu_ln_b, ssd_dt_bias, ssd_a_log, ssd_d,
                        ssd_norm_g, ssd_conv_b, sconv_w, ssd_conv_w):
    depth = norm_mix_g.shape[0]

    def rep(v):
        return jnp.repeat(v, GROUP_W // SSD_HEADS, axis=-1)

    def pad_to_row(v):
        return jnp.pad(v, [(0, 0)] * (v.ndim - 1) + [(0, D_MODEL - v.shape[-1])])

    rows = [
        norm_mix_g,
        jnp.concatenate([pool_b, pool_scale, sgu_ln_g, sgu_ln_b], axis=-1),
        jnp.concatenate([rep(ssd_dt_bias), rep(ssd_a_log), rep(ssd_d), ssd_norm_g], axis=-1),
        pad_to_row(ssd_conv_b),
        pad_to_row(sconv_w.reshape(depth, SCONV_WIDTH * GROUP_W)),
    ]
    table = jnp.concatenate([r[:, None, :] for r in rows] + [pad_to_row(ssd_conv_w)], axis=1)
    assert table.shape == (depth, N_VEC_ROWS, D_MODEL)
    return table.astype(F32)


def kernel(x, norm_mix_g, w_in, pool_w, pool_b, pool_scale, sgu_ln_g, sgu_ln_b, sgu_w, sgu_b, sconv_w, ssd_conv_w, ssd_conv_b, ssd_dt_bias, ssd_a_log, ssd_d, ssd_norm_g, w_out, norm_ffn_g, w_gate, w_up, w_down, final_norm_g):
    b, s, d = x.shape
    depth = w_in.shape[0]
    vecs = _pack_layer_vectors(norm_mix_g, pool_b, pool_scale, sgu_ln_g, sgu_ln_b, ssd_dt_bias, ssd_a_log,
                               ssd_d, ssd_norm_g, ssd_conv_b, sconv_w, ssd_conv_w)
    win = w_in.astype(BF16)
    wdt = jnp.repeat(w_in[:, :, C_DT:], GROUP_W // SSD_HEADS, axis=2).astype(BF16)
    poolw = jnp.einsum('lgcd,gh->lgchd', pool_w, jnp.eye(POOL_GROUPS, dtype=F32)).reshape(
        depth, GROUP_W, GROUP_W).astype(BF16)
    sgub = jnp.repeat(jnp.swapaxes(sgu_b, 1, 2), GROUP_W // SGU_HEADS, axis=2)
    wout, wd = w_out.astype(BF16), w_down.astype(BF16)
    fg = final_norm_g.reshape(1, d).astype(F32)
    mixer_params = (vecs, win, wdt, poolw, sgu_w, sgub, wout)

    h = x.reshape(b * s, d)
    for layer in range(depth):
        h = _mixer_call(h, s, layer, mixer_params)
        h = _ffn_call(h, layer, norm_ffn_g.reshape(depth, 1, d), w_gate, w_up, wd, fg,
                      final_norm=(layer == depth - 1))
    return h.reshape(b, s, d)
```

```python
import functools

import jax
import jax.numpy as jnp
from jax import lax
from jax.experimental import pallas as pl
from jax.experimental.pallas import tpu as pltpu

F32 = jnp.float32
BF16 = jnp.bfloat16

D_MODEL = 1024
GROUP_W = 256
POOL_GROUPS = 4
SGU_BLOCK = 128
SGU_HEADS = 4
HALO = 8
SCONV_WIDTH = 3
SSD_CONV = 4
SSD_CHUNK = 64
SSD_STATE = 128
SSD_GROUPS = 2
SSD_HEADS = 4
SSD_XBC = 768
D_FF = 2816
EPS = 1e-6

C_POOL, C_U, C_V, C_CB, C_CC, C_CH, C_Z, C_XBC, C_DT = 0, 256, 512, 768, 1024, 1280, 1536, 1792, 2560
PROJ_COLS = C_DT + GROUP_W

V_NORM_G, V_POOL_SGU, V_SSD, V_CONV_B, V_SCONV_W, V_CONV_W = 0, 1, 2, 3, 4, 5
N_VEC_ROWS = V_CONV_W + SSD_CONV

SEQ_TILE = 1024
SUB_TILE = 512
N_SUB = SEQ_TILE // SUB_TILE
assert N_SUB == 2
FFN_TILE = 1024
FFN_CHUNK = 256
MIB = 1024 * 1024
MIXER_VMEM_LIMIT_BYTES = 56 * MIB
FFN_VMEM_LIMIT_BYTES = 60 * MIB
PROJ_PIECES_AFTER_MIX_PIECE = (2, 1, 1, 1, 1, 1, 1, 1, 1, 1, 1, 0)


def _dot(a, b):
    return jnp.dot(a, b, preferred_element_type=F32)


NEG_LOG2_E = -1.4426950408889634


def _silu(x):
    return x * (1.0 / (1.0 + jnp.exp2(x * NEG_LOG2_E)))


def _rms_scale(x):
    return x * lax.rsqrt(jnp.mean(x * x, axis=-1, keepdims=True) + EPS)


def _proj_stage(h_ref, row0, vec_ref, win_ref, wdt_ref, hn_buf, proj_buf):
    g = vec_ref[V_NORM_G:V_NORM_G + 1, :]
    hn_buf[...] = (_rms_scale(h_ref[row0:row0 + SUB_TILE, :]) * g).astype(BF16)
    yield
    for c0 in range(0, C_DT, GROUP_W):
        proj_buf[:, c0:c0 + GROUP_W] = _dot(hn_buf[...], win_ref[:, c0:c0 + GROUP_W])
        yield
    proj_buf[:, C_DT:C_DT + GROUP_W] = _dot(hn_buf[...], wdt_ref[...])
    yield


def _mix_stage(k, pos0, h_ref, row0, out_ref, proj, prm, carry):
    vec_ref, poolw_ref, sguw_ref, sgub_ref, wout_ref = prm
    x_buf, s2_buf, s4_buf, s8_buf, sc_buf, xbc_buf, state_ref = carry
    ts = SUB_TILE
    nxt = (k + 1) % N_SUB
    rows = slice(row0, row0 + ts)
    n_chunks = ts // SSD_CHUNK

    def vec(row, slot, width=GROUP_W):
        return vec_ref[row:row + 1, slot * GROUP_W:slot * GROUP_W + width]

    def out_dot(m, y):
        return _dot(y.astype(BF16), wout_ref[m * GROUP_W:(m + 1) * GROUP_W, :])

    def with_halo(buf, val):
        buf[k, HALO:HALO + ts, :] = val
        buf[nxt, 0:HALO, :] = val[ts - HALO:ts, :]

    lane = lax.broadcasted_iota(jnp.int32, (1, 128), 1)
    lo_half = lane < 64

    xbc_buf[nxt, 0:HALO, :] = proj[ts - HALO:ts, C_XBC:C_XBC + SSD_XBC]
    xbc = []
    for part in range(SSD_XBC // GROUP_W):
        cols = slice(part * GROUP_W, (part + 1) * GROUP_W)
        pre = proj[:, C_XBC + part * GROUP_W:C_XBC + (part + 1) * GROUP_W]
        xbc_buf[k, HALO:HALO + ts, cols] = pre
        conv = pre * vec_ref[V_CONV_W + SSD_CONV - 1:V_CONV_W + SSD_CONV, cols] + vec_ref[V_CONV_B:V_CONV_B + 1, cols]
        for lag in range(1, SSD_CONV):
            tap = V_CONV_W + SSD_CONV - 1 - lag
            conv = conv + xbc_buf[k, HALO - lag:HALO - lag + ts, cols] * vec_ref[tap:tap + 1, cols]
        xbc.append(_silu(conv))
        yield
    xh, bmat, cmat = xbc

    dt = proj[:, C_DT:C_DT + GROUP_W] + vec(V_SSD, 0)
    delta = jnp.maximum(dt, 0.0) + jnp.log(1.0 + jnp.exp(-jnp.abs(dt)))
    da = delta * (-jnp.exp(vec(V_SSD, 1)))
    x_dt = xh * delta

    rr = lax.broadcasted_iota(jnp.int32, (SSD_CHUNK, 3 * SSD_CHUNK), 0)
    cc = lax.broadcasted_iota(jnp.int32, (SSD_CHUNK, 3 * SSD_CHUNK), 1) % SSD_CHUNK
    tri3 = jnp.where(cc <= rr, 1.0, 0.0).astype(BF16)
    da_hi = da.astype(BF16)
    rem = da - da_hi.astype(F32)
    da_mid = rem.astype(BF16)
    da_lo = (rem - da_mid.astype(F32)).astype(BF16)
    acs_chunks = []
    for c in range(n_chunks):
        crow = slice(c * SSD_CHUNK, (c + 1) * SSD_CHUNK)
        acs_chunks.append(_dot(tri3, jnp.concatenate([da_hi[crow], da_mid[crow], da_lo[crow]], axis=0)))
    yield

    v = proj[:, C_V:C_V + GROUP_W]
    mu = jnp.mean(v, axis=-1, keepdims=True)
    vc = v - mu
    var = jnp.mean(vc * vc, axis=-1, keepdims=True)
    vn = ((vc * lax.rsqrt(var + EPS)) * vec(V_POOL_SGU, 2) + vec(V_POOL_SGU, 3)).astype(BF16)
    ri = lax.broadcasted_iota(jnp.int32, (SGU_BLOCK, SGU_BLOCK), 0) // SSD_CHUNK
    ci = lax.broadcasted_iota(jnp.int32, (SGU_BLOCK, SGU_BLOCK), 1) // SSD_CHUNK
    chunk_causal = ri >= ci
    lane256 = lax.broadcasted_iota(jnp.int32, (1, GROUP_W), 1)
    w_heads = [jnp.where(chunk_causal, sguw_ref[hd], 0.0).astype(BF16) for hd in range(SGU_HEADS)]
    head_mix = [[_dot(w_heads[hd], vn[blk * SGU_BLOCK:(blk + 1) * SGU_BLOCK, :]) for hd in range(SGU_HEADS)]
                for blk in range(ts // SGU_BLOCK)]
    yield

    row64 = lax.broadcasted_iota(jnp.int32, (SSD_CHUNK, 128), 0)
    col64 = lax.broadcasted_iota(jnp.int32, (SSD_CHUNK, 128), 1) % SSD_CHUNK
    causal2 = col64 <= row64
    blocks = [(c, g) for c in range(n_chunks) for g in range(SSD_GROUPS)]

    def blk(arr, c, g):
        return arr[c * SSD_CHUNK:(c + 1) * SSD_CHUNK, g * 128:(g + 1) * 128]

    a2, bt_bf, upd = {}, {}, {}
    for c, g in blocks:
        a2[c, g] = acs_chunks[c][:, g * 128:(g + 1) * 128]
        bt_bf[c, g] = blk(bmat, c, g).T.astype(BF16)
        last = a2[c, g][SSD_CHUNK - 1:SSD_CHUNK, :]
        upd[c, g] = _dot(bt_bf[c, g], (blk(x_dt, c, g) * jnp.exp(last - a2[c, g])).astype(BF16))
    yield

    xp = proj[:, C_POOL:C_POOL + GROUP_W]
    with_halo(x_buf, xp)
    s2 = xp + x_buf[k, HALO - 1:HALO - 1 + ts, :]
    with_halo(s2_buf, s2)
    s4 = s2 + s2_buf[k, HALO - 2:HALO - 2 + ts, :]
    with_halo(s4_buf, s4[:, 128:256])
    s8 = s4[:, 128:256] + s4_buf[k, HALO - 4:HALO - 4 + ts, :]
    with_halo(s8_buf, s8)
    s16 = s8 + s8_buf[k, 0:ts, :]
    pos1 = (lax.broadcasted_iota(jnp.int32, (ts, 128), 0) + (pos0 + 1)).astype(F32)
    cnt_a = jnp.where(lo_half, jnp.minimum(pos1, 2.0), jnp.minimum(pos1, 4.0))
    cnt_b = jnp.where(lo_half, jnp.minimum(pos1, 8.0), jnp.minimum(pos1, 16.0))
    pooled = jnp.concatenate([jnp.where(lo_half, s2[:, 0:128], s4[:, 0:128]) / cnt_a,
                              jnp.where(lo_half, s8, s16) / cnt_b], axis=1) - xp
    pool_mm = _dot(pooled.astype(BF16), poolw_ref[...])

    mixed_blocks = []
    for per_head in head_mix:
        mixed = per_head[0]
        for hd in range(1, SGU_HEADS):
            mixed = jnp.where(lane256 >= hd * 64, per_head[hd], mixed)
        mixed_blocks.append(mixed + sgub_ref[...])
    y_b = proj[:, C_U:C_U + GROUP_W] * jnp.concatenate(mixed_blocks, axis=0)
    d_b = out_dot(1, y_b)
    yield

    states = [state_ref[g] for g in range(SSD_GROUPS)]
    both = {}
    for c, g in blocks:
        rhs = jnp.concatenate([bt_bf[c, g], bt_bf[c, g], states[g].astype(BF16)], axis=1)
        both[c, g] = _dot(blk(cmat, c, g).astype(BF16), rhs)
        last = a2[c, g][SSD_CHUNK - 1:SSD_CHUNK, :]
        states[g] = states[g] * jnp.exp(last) + upd[c, g]
    for g in range(SSD_GROUPS):
        state_ref[g] = states[g]
    yield

    prod = proj[:, C_CC:C_CC + GROUP_W] * proj[:, C_CH:C_CH + GROUP_W]
    with_halo(sc_buf, prod)
    conv = prod * vec(V_SCONV_W, SCONV_WIDTH - 1)
    for lag in range(1, SCONV_WIDTH):
        conv = conv + sc_buf[k, HALO - lag:HALO - lag + ts, :] * vec(V_SCONV_W, SCONV_WIDTH - 1 - lag)
    y_c = proj[:, C_CB:C_CB + GROUP_W] * conv
    d_c = out_dot(2, y_c)
    y_a = (pool_mm + vec(V_POOL_SGU, 0)) * vec(V_POOL_SGU, 1)
    d_a = out_dot(0, y_a)
    out_ref[rows, :] = h_ref[rows, :] + d_b
    yield

    y_diag = {}
    for c, g in blocks:
        a_key = jnp.sum(jnp.where(row64 <= col64, blk(da, c, g), 0.0), axis=0, keepdims=True)
        decay = jnp.exp(jnp.where(causal2, a2[c, g] - a_key, -jnp.inf))
        x2 = blk(x_dt, c, g)
        x_bd = jnp.concatenate([jnp.where(lo_half, x2, 0.0), jnp.where(lo_half, 0.0, x2)], axis=0)
        y_diag[c, g] = _dot((both[c, g][:, 0:128] * decay).astype(BF16), x_bd.astype(BF16))
    out_ref[rows, :] += d_a + d_c
    yield

    y_ssd = jnp.concatenate(
        [jnp.concatenate([y_diag[c, g] + both[c, g][:, 128:256] * jnp.exp(a2[c, g]) for g in range(SSD_GROUPS)], axis=1)
         for c in range(n_chunks)], axis=0)
    y = (y_ssd + xh * vec(V_SSD, 2)) * _silu(proj[:, C_Z:C_Z + GROUP_W])
    y_d = jnp.concatenate([_rms_scale(y[:, 0:128]), _rms_scale(y[:, 128:256])], axis=1) * vec(V_SSD, 3)
    d_d = out_dot(3, y_d)
    yield
    out_ref[rows, :] += d_d
    yield


def _interleave(main, side):
    for n_side in PROJ_PIECES_AFTER_MIX_PIECE:
        next(main)
        for _ in range(n_side):
            next(side)
    assert next(main, "done") == "done" and next(side, "done") == "done"


def _mixer_kernel(h_ref, hnext_ref, vec_ref, win_ref, wdt_ref, poolw_ref, sguw_ref, sgub_ref, wout_ref,
                  out_ref, hn_buf, proj_a, proj_b, x_buf, s2_buf, s4_buf, s8_buf, sc_buf, xbc_buf,
                  state_ref, *, tiles_per_seq):
    prm = (vec_ref, poolw_ref, sguw_ref, sgub_ref, wout_ref)
    halo_bufs = (x_buf, s2_buf, s4_buf, s8_buf, sc_buf, xbc_buf)
    carry = halo_bufs + (state_ref,)
    step = pl.program_id(0)
    tile_in_seq = step % tiles_per_seq

    def proj_stage(src_ref, row0, dst):
        return _proj_stage(src_ref, row0, vec_ref, win_ref, wdt_ref, hn_buf, dst)

    @pl.when(step == 0)
    def _():
        for _ in proj_stage(h_ref, 0, proj_a):
            pass

    @pl.when(tile_in_seq == 0)
    def _():
        for buf in halo_bufs:
            buf[0, 0:HALO, :] = jnp.zeros((HALO, buf.shape[2]), F32)
        state_ref[...] = jnp.zeros(state_ref.shape, F32)

    pos0 = tile_in_seq * SEQ_TILE
    _interleave(_mix_stage(0, pos0, h_ref, 0, out_ref, proj_a, prm, carry),
                proj_stage(h_ref, SUB_TILE, proj_b))
    _interleave(_mix_stage(1, pos0 + SUB_TILE, h_ref, SUB_TILE, out_ref, proj_b, prm, carry),
                proj_stage(hnext_ref, 0, proj_a))


def _ffn_kernel(h_ref, g_ref, wg_ref, wu_ref, wd_ref, fg_ref, out_ref, hn_ref, acc_ref, *, final_norm):
    h = h_ref[...]
    hn_ref[...] = (_rms_scale(h) * g_ref[...]).astype(BF16)
    acc_ref[...] = h
    for j in range(D_FF // FFN_CHUNK):
        c0 = j * FFN_CHUNK
        hn = hn_ref[...]
        gate = _dot(hn, wg_ref[:, c0:c0 + FFN_CHUNK].astype(BF16))
        up = _dot(hn, wu_ref[:, c0:c0 + FFN_CHUNK].astype(BF16))
        act = (_silu(gate) * up).astype(BF16)
        acc_ref[...] += _dot(act, wd_ref[c0:c0 + FFN_CHUNK, :].astype(BF16))
    res = acc_ref[...]
    if final_norm:
        res = _rms_scale(res) * fg_ref[...]
    out_ref[...] = res


def _layer_spec(arr, layer):
    zeros = (0,) * (arr.ndim - 1)
    return pl.BlockSpec((None,) + arr.shape[1:], lambda i: (layer,) + zeros)


def _mixer_call(h2d, seq_len, layer, params):
    t, d = h2d.shape
    n_tiles = t // SEQ_TILE
    assert seq_len % SEQ_TILE == 0 and t % seq_len == 0
    tile_spec = pl.BlockSpec((SEQ_TILE, d), lambda i: (i, 0))
    next_spec = pl.BlockSpec((SUB_TILE, d), lambda i: (jnp.minimum(i + 1, n_tiles - 1) * N_SUB, 0))
    return pl.pallas_call(
        functools.partial(_mixer_kernel, tiles_per_seq=seq_len // SEQ_TILE),
        grid=(n_tiles,),
        in_specs=[tile_spec, next_spec] + [_layer_spec(p, layer) for p in params],
        out_specs=tile_spec,
        out_shape=jax.ShapeDtypeStruct(h2d.shape, F32),
        scratch_shapes=[
            pltpu.VMEM((SUB_TILE, d), BF16),
            pltpu.VMEM((SUB_TILE, PROJ_COLS), F32),
            pltpu.VMEM((SUB_TILE, PROJ_COLS), F32),
            pltpu.VMEM((N_SUB, SUB_TILE + HALO, GROUP_W), F32),
            pltpu.VMEM((N_SUB, SUB_TILE + HALO, GROUP_W), F32),
            pltpu.VMEM((N_SUB, SUB_TILE + HALO, 128), F32),
            pltpu.VMEM((N_SUB, SUB_TILE + HALO, 128), F32),
            pltpu.VMEM((N_SUB, SUB_TILE + HALO, GROUP_W), F32),
            pltpu.VMEM((N_SUB, SUB_TILE + HALO, SSD_XBC), F32),
            pltpu.VMEM((SSD_GROUPS, 128, SSD_STATE), F32),
        ],
        compiler_params=pltpu.CompilerParams(
            dimension_semantics=("arbitrary",),
            vmem_limit_bytes=MIXER_VMEM_LIMIT_BYTES),
        name="mixer",
    )(h2d, h2d, *params)


def _ffn_call(h2d, layer, g, wg, wu, wd, fg, final_norm):
    t, d = h2d.shape
    tile_spec = pl.BlockSpec((FFN_TILE, d), lambda i: (i, 0))
    row_spec = pl.BlockSpec((1, d), lambda i: (0, 0))
    return pl.pallas_call(
        functools.partial(_ffn_kernel, final_norm=final_norm),
        grid=(t // FFN_TILE,),
        in_specs=[tile_spec, _layer_spec(g, layer), _layer_spec(wg, layer),
                  _layer_spec(wu, layer), _layer_spec(wd, layer), row_spec],
        out_specs=tile_spec,
        out_shape=jax.ShapeDtypeStruct(h2d.shape, F32),
        scratch_shapes=[pltpu.VMEM((FFN_TILE, d), BF16), pltpu.VMEM((FFN_TILE, d), F32)],
        compiler_params=pltpu.CompilerParams(
            dimension_semantics=("parallel",),
            vmem_limit_bytes=FFN_VMEM_LIMIT_BYTES),
        name="ffn",
    )(h2d, g, wg, wu, wd, fg)


def _pack_layer_vectors(norm_mix_g, pool_b, pool_scale, sgu_ln_g, sgu_ln_b, ssd_dt_bias, ssd_a_log, ssd_d,
                        ssd_norm_g, ssd_conv_b, sconv_w, ssd_conv_w):
    depth = norm_mix_g.shape[0]

    def rep(v):
        return jnp.repeat(v, GROUP_W // SSD_HEADS, axis=-1)

    def pad_to_row(v):
        return jnp.pad(v, [(0, 0)] * (v.ndim - 1) + [(0, D_MODEL - v.shape[-1])])

    rows = [
        norm_mix_g,
        jnp.concatenate([pool_b, pool_scale, sgu_ln_g, sgu_ln_b], axis=-1),
        jnp.concatenate([rep(ssd_dt_bias), rep(ssd_a_log), rep(ssd_d), ssd_norm_g], axis=-1),
        pad_to_row(ssd_conv_b),
        pad_to_row(sconv_w.reshape(depth, SCONV_WIDTH * GROUP_W)),
    ]
    table = jnp.concatenate([r[:, None, :] for r in rows] + [pad_to_row(ssd_conv_w)], axis=1)
    assert table.shape == (depth, N_VEC_ROWS, D_MODEL)
    return table.astype(F32)


def kernel(x, norm_mix_g, w_in, pool_w, pool_b, pool_scale, sgu_ln_g, sgu_ln_b, sgu_w, sgu_b, sconv_w, ssd_conv_w, ssd_conv_b, ssd_dt_bias, ssd_a_log, ssd_d, ssd_norm_g, w_out, norm_ffn_g, w_gate, w_up, w_down, final_norm_g):
    b, s, d = x.shape
    depth = w_in.shape[0]
    vecs = _pack_layer_vectors(norm_mix_g, pool_b, pool_scale, sgu_ln_g, sgu_ln_b, ssd_dt_bias, ssd_a_log,
                               ssd_d, ssd_norm_g, ssd_conv_b, sconv_w, ssd_conv_w)
    win = w_in.astype(BF16)
    wdt = jnp.repeat(w_in[:, :, C_DT:], GROUP_W // SSD_HEADS, axis=2).astype(BF16)
    poolw = jnp.einsum('lgcd,gh->lgchd', pool_w, jnp.eye(POOL_GROUPS, dtype=F32)).reshape(
        depth, GROUP_W, GROUP_W).astype(BF16)
    sgub = jnp.repeat(jnp.swapaxes(sgu_b, 1, 2), GROUP_W // SGU_HEADS, axis=2)
    wout = w_out.astype(BF16)
    fg = final_norm_g.reshape(1, d).astype(F32)
    mixer_params = (vecs, win, wdt, poolw, sgu_w, sgub, wout)

    h = x.reshape(b * s, d)
    for layer in range(depth):
        h = _mixer_call(h, s, layer, mixer_params)
        h = _ffn_call(h, layer, norm_ffn_g.reshape(depth, 1, d), w_gate, w_up, w_down, fg,
                      final_norm=(layer == depth - 1))
    return h.reshape(b, s, d)
```

```python
import functools

import jax
import jax.numpy as jnp
from jax import lax
from jax.experimental import pallas as pl
from jax.experimental.pallas import tpu as pltpu

F32 = jnp.float32
BF16 = jnp.bfloat16

D_MODEL = 1024
GROUP_W = 256
POOL_GROUPS = 4
SGU_BLOCK = 128
SGU_HEADS = 4
HALO = 8
SCONV_WIDTH = 3
SSD_CONV = 4
SSD_CHUNK = 64
SSD_STATE = 128
SSD_GROUPS = 2
SSD_HEADS = 4
SSD_XBC = 768
SLAB_XBC, SLAB_DT, SLAB_Z, N_SSD_SLABS = 0, 6, 8, 10
D_FF = 2816
EPS = 1e-6

C_POOL, C_U, C_V, C_CB, C_CC, C_CH, C_Z, C_XBC, C_DT = 0, 256, 512, 768, 1024, 1280, 1536, 1792, 2560
PROJ_COLS = C_DT + GROUP_W

V_NORM_G, V_POOL_SGU, V_SSD, V_CONV_B, V_SCONV_W, V_CONV_W = 0, 1, 2, 3, 4, 5
N_VEC_ROWS = V_CONV_W + SSD_CONV

SEQ_TILE = 1024
SUB_TILE = 512
N_SUB = SEQ_TILE // SUB_TILE
assert N_SUB == 2
FFN_TILE = 1024
FFN_CHUNK = 256
MIB = 1024 * 1024
MIXER_VMEM_LIMIT_BYTES = 56 * MIB
FFN_VMEM_LIMIT_BYTES = 60 * MIB
PROJ_PIECES_AFTER_MIX_PIECE = (2, 1, 1, 1, 1, 1, 1, 1, 1, 1, 1, 0)


def _dot(a, b):
    return jnp.dot(a, b, preferred_element_type=F32)


NEG_LOG2_E = -1.4426950408889634


def _silu(x):
    return x * (1.0 / (1.0 + jnp.exp2(x * NEG_LOG2_E)))


def _rms_scale(x):
    return x * lax.rsqrt(jnp.mean(x * x, axis=-1, keepdims=True) + EPS)


def _proj_stage(h_ref, row0, vec_ref, win_ref, wdt_ref, hn_buf, proj_buf):
    g = vec_ref[V_NORM_G:V_NORM_G + 1, :]
    hn_buf[...] = (_rms_scale(h_ref[row0:row0 + SUB_TILE, :]) * g).astype(BF16)
    yield
    for c0 in range(0, C_DT, GROUP_W):
        proj_buf[:, c0:c0 + GROUP_W] = _dot(hn_buf[...], win_ref[:, c0:c0 + GROUP_W])
        yield
    proj_buf[:, C_DT:C_DT + GROUP_W] = _dot(hn_buf[...], wdt_ref[...])
    yield


def _mix_stage(k, pos0, h_ref, row0, out_ref, proj, prm, carry):
    vec_ref, poolw_ref, sguw_ref, sgub_ref, wout_ref = prm
    x_buf, s2_buf, s4_buf, s8_buf, sc_buf, ssd_slab, y_nat, state_ref = carry
    ts = SUB_TILE
    nxt = (k + 1) % N_SUB
    rows = slice(row0, row0 + ts)
    n_chunks = ts // SSD_CHUNK

    def vec(row, slot, width=GROUP_W):
        return vec_ref[row:row + 1, slot * GROUP_W:slot * GROUP_W + width]

    def out_dot(m, y):
        return _dot(y.astype(BF16), wout_ref[m * GROUP_W:(m + 1) * GROUP_W, :])

    def with_halo(buf, val):
        buf[k, HALO:HALO + ts, :] = val
        buf[nxt, 0:HALO, :] = val[ts - HALO:ts, :]

    lane = lax.broadcasted_iota(jnp.int32, (1, 128), 1)
    lo_half = lane < 64

    half = SSD_CHUNK // 2

    def chunk_order(even, odd):
        return jnp.concatenate([part[c * half:(c + 1) * half] for c in range(n_chunks) for part in (even, odd)], axis=0)

    def slab_rows(slab, first_row):
        return ssd_slab[k, slab, pl.ds(HALO + first_row, ts // 2, stride=2), :]

    def store_slabs(slab0, c0, width):
        for j in range(width // 128):
            cols = proj[:, c0 + j * 128:c0 + (j + 1) * 128]
            ssd_slab[k, slab0 + j, HALO:HALO + ts, :] = cols
            if slab0 == SLAB_XBC:
                ssd_slab[nxt, slab0 + j, 0:HALO, :] = cols[ts - HALO:ts, :]

    def in_chunk_order(slab0, width):
        return jnp.concatenate([chunk_order(slab_rows(slab0 + j, 0), slab_rows(slab0 + j, 1))
                                for j in range(width // 128)], axis=1)

    store_slabs(SLAB_XBC, C_XBC, SSD_XBC)
    xbc = []
    for part in range(SSD_XBC // GROUP_W):
        halves = []
        for j in range(GROUP_W // 128):
            slab = SLAB_XBC + 2 * part + j
            cols = slice(slab * 128, (slab + 1) * 128)
            parity = []
            for p in range(2):
                conv = slab_rows(slab, p) * vec_ref[V_CONV_W + SSD_CONV - 1:V_CONV_W + SSD_CONV, cols] \
                    + vec_ref[V_CONV_B:V_CONV_B + 1, cols]
                for lag in range(1, SSD_CONV):
                    tap = V_CONV_W + SSD_CONV - 1 - lag
                    conv = conv + slab_rows(slab, p - lag) * vec_ref[tap:tap + 1, cols]
                parity.append(_silu(conv))
            halves.append(chunk_order(*parity))
        xbc.append(jnp.concatenate(halves, axis=1))
        yield
    xh, bmat, cmat = xbc

    store_slabs(SLAB_DT, C_DT, GROUP_W)
    store_slabs(SLAB_Z, C_Z, GROUP_W)
    dt = in_chunk_order(SLAB_DT, GROUP_W) + vec(V_SSD, 0)
    delta = jnp.maximum(dt, 0.0) + jnp.log(1.0 + jnp.exp(-jnp.abs(dt)))
    da = delta * (-jnp.exp(vec(V_SSD, 1)))
    x_dt = xh * delta

    def frame_of(pos):
        return (2 * pos) % SSD_CHUNK + pos // half

    rr = frame_of(lax.broadcasted_iota(jnp.int32, (SSD_CHUNK, 3 * SSD_CHUNK), 0))
    cc = frame_of(lax.broadcasted_iota(jnp.int32, (SSD_CHUNK, 3 * SSD_CHUNK), 1) % SSD_CHUNK)
    tri3 = jnp.where(cc <= rr, 1.0, 0.0).astype(BF16)
    da_hi = da.astype(BF16)
    rem = da - da_hi.astype(F32)
    da_mid = rem.astype(BF16)
    da_lo = (rem - da_mid.astype(F32)).astype(BF16)
    acs_chunks = []
    for c in range(n_chunks):
        crow = slice(c * SSD_CHUNK, (c + 1) * SSD_CHUNK)
        acs_chunks.append(_dot(tri3, jnp.concatenate([da_hi[crow], da_mid[crow], da_lo[crow]], axis=0)))
    yield

    v = proj[:, C_V:C_V + GROUP_W]
    mu = jnp.mean(v, axis=-1, keepdims=True)
    vc = v - mu
    var = jnp.mean(vc * vc, axis=-1, keepdims=True)
    vn = ((vc * lax.rsqrt(var + EPS)) * vec(V_POOL_SGU, 2) + vec(V_POOL_SGU, 3)).astype(BF16)
    ri = lax.broadcasted_iota(jnp.int32, (SGU_BLOCK, SGU_BLOCK), 0) // SSD_CHUNK
    ci = lax.broadcasted_iota(jnp.int32, (SGU_BLOCK, SGU_BLOCK), 1) // SSD_CHUNK
    chunk_causal = ri >= ci
    lane256 = lax.broadcasted_iota(jnp.int32, (1, GROUP_W), 1)
    w_heads = [jnp.where(chunk_causal, sguw_ref[hd], 0.0).astype(BF16) for hd in range(SGU_HEADS)]
    head_mix = [[_dot(w_heads[hd], vn[blk * SGU_BLOCK:(blk + 1) * SGU_BLOCK, :]) for hd in range(SGU_HEADS)]
                for blk in range(ts // SGU_BLOCK)]
    yield

    row64 = frame_of(lax.broadcasted_iota(jnp.int32, (SSD_CHUNK, 128), 0))
    col64 = frame_of(lax.broadcasted_iota(jnp.int32, (SSD_CHUNK, 128), 1) % SSD_CHUNK)
    causal2 = col64 <= row64
    blocks = [(c, g) for c in range(n_chunks) for g in range(SSD_GROUPS)]

    def blk(arr, c, g):
        return arr[c * SSD_CHUNK:(c + 1) * SSD_CHUNK, g * 128:(g + 1) * 128]

    a2, bt_bf, upd = {}, {}, {}
    for c, g in blocks:
        a2[c, g] = acs_chunks[c][:, g * 128:(g + 1) * 128]
        bt_bf[c, g] = blk(bmat, c, g).T.astype(BF16)
        last = a2[c, g][SSD_CHUNK - 1:SSD_CHUNK, :]
        upd[c, g] = _dot(bt_bf[c, g], (blk(x_dt, c, g) * jnp.exp(last - a2[c, g])).astype(BF16))
    yield

    xp = proj[:, C_POOL:C_POOL + GROUP_W]
    with_halo(x_buf, xp)
    s2 = xp + x_buf[k, HALO - 1:HALO - 1 + ts, :]
    with_halo(s2_buf, s2)
    s4 = s2 + s2_buf[k, HALO - 2:HALO - 2 + ts, :]
    with_halo(s4_buf, s4[:, 128:256])
    s8 = s4[:, 128:256] + s4_buf[k, HALO - 4:HALO - 4 + ts, :]
    with_halo(s8_buf, s8)
    s16 = s8 + s8_buf[k, 0:ts, :]
    pos1 = (lax.broadcasted_iota(jnp.int32, (ts, 128), 0) + (pos0 + 1)).astype(F32)
    cnt_a = jnp.where(lo_half, jnp.minimum(pos1, 2.0), jnp.minimum(pos1, 4.0))
    cnt_b = jnp.where(lo_half, jnp.minimum(pos1, 8.0), jnp.minimum(pos1, 16.0))
    pooled = jnp.concatenate([jnp.where(lo_half, s2[:, 0:128], s4[:, 0:128]) / cnt_a,
                              jnp.where(lo_half, s8, s16) / cnt_b], axis=1) - xp
    pool_mm = _dot(pooled.astype(BF16), poolw_ref[...])

    mixed_blocks = []
    for per_head in head_mix:
        mixed = per_head[0]
        for hd in range(1, SGU_HEADS):
            mixed = jnp.where(lane256 >= hd * 64, per_head[hd], mixed)
        mixed_blocks.append(mixed + sgub_ref[...])
    y_b = proj[:, C_U:C_U + GROUP_W] * jnp.concatenate(mixed_blocks, axis=0)
    d_b = out_dot(1, y_b)
    yield

    states = [state_ref[g] for g in range(SSD_GROUPS)]
    both = {}
    for c, g in blocks:
        rhs = jnp.concatenate([bt_bf[c, g], bt_bf[c, g], states[g].astype(BF16)], axis=1)
        both[c, g] = _dot(blk(cmat, c, g).astype(BF16), rhs)
        last = a2[c, g][SSD_CHUNK - 1:SSD_CHUNK, :]
        states[g] = states[g] * jnp.exp(last) + upd[c, g]
    for g in range(SSD_GROUPS):
        state_ref[g] = states[g]
    yield

    prod = proj[:, C_CC:C_CC + GROUP_W] * proj[:, C_CH:C_CH + GROUP_W]
    with_halo(sc_buf, prod)
    conv = prod * vec(V_SCONV_W, SCONV_WIDTH - 1)
    for lag in range(1, SCONV_WIDTH):
        conv = conv + sc_buf[k, HALO - lag:HALO - lag + ts, :] * vec(V_SCONV_W, SCONV_WIDTH - 1 - lag)
    y_c = proj[:, C_CB:C_CB + GROUP_W] * conv
    d_c = out_dot(2, y_c)
    y_a = (pool_mm + vec(V_POOL_SGU, 0)) * vec(V_POOL_SGU, 1)
    d_a = out_dot(0, y_a)
    out_ref[rows, :] = h_ref[rows, :] + d_b
    yield

    y_diag = {}
    for c, g in blocks:
        a_key = jnp.sum(jnp.where(row64 <= col64, blk(da, c, g), 0.0), axis=0, keepdims=True)
        decay = jnp.exp(jnp.where(causal2, a2[c, g] - a_key, -jnp.inf))
        x2 = blk(x_dt, c, g)
        x_bd = jnp.concatenate([jnp.where(lo_half, x2, 0.0), jnp.where(lo_half, 0.0, x2)], axis=0)
        y_diag[c, g] = _dot((both[c, g][:, 0:128] * decay).astype(BF16), x_bd.astype(BF16))
    out_ref[rows, :] += d_a + d_c
    yield

    y_ssd = jnp.concatenate(
        [jnp.concatenate([y_diag[c, g] + both[c, g][:, 128:256] * jnp.exp(a2[c, g]) for g in range(SSD_GROUPS)], axis=1)
         for c in range(n_chunks)], axis=0)
    y = (y_ssd + xh * vec(V_SSD, 2)) * _silu(in_chunk_order(SLAB_Z, GROUP_W))
    y_d = jnp.concatenate([_rms_scale(y[:, 0:128]), _rms_scale(y[:, 128:256])], axis=1) * vec(V_SSD, 3)
    for j in range(GROUP_W // 128):
        for c in range(n_chunks):
            for p in range(2):
                r0 = c * SSD_CHUNK + p * half
                y_nat[k, j, pl.ds(c * SSD_CHUNK + p, half, stride=2), :] = y_d[r0:r0 + half, j * 128:(j + 1) * 128]
    d_d = out_dot(3, jnp.concatenate([y_nat[k, j] for j in range(GROUP_W // 128)], axis=1))
    yield
    out_ref[rows, :] += d_d
    yield


def _interleave(main, side):
    for n_side in PROJ_PIECES_AFTER_MIX_PIECE:
        next(main)
        for _ in range(n_side):
            next(side)
    assert next(main, "done") == "done" and next(side, "done") == "done"


def _mixer_kernel(h_ref, hnext_ref, vec_ref, win_ref, wdt_ref, poolw_ref, sguw_ref, sgub_ref, wout_ref,
                  out_ref, hn_buf, proj_a, proj_b, x_buf, s2_buf, s4_buf, s8_buf, sc_buf, ssd_slab, y_nat,
                  state_ref, *, tiles_per_seq):
    prm = (vec_ref, poolw_ref, sguw_ref, sgub_ref, wout_ref)
    halo_bufs = (x_buf, s2_buf, s4_buf, s8_buf, sc_buf)
    carry = halo_bufs + (ssd_slab, y_nat, state_ref)
    step = pl.program_id(0)
    tile_in_seq = step % tiles_per_seq

    def proj_stage(src_ref, row0, dst):
        return _proj_stage(src_ref, row0, vec_ref, win_ref, wdt_ref, hn_buf, dst)

    @pl.when(step == 0)
    def _():
        for _ in proj_stage(h_ref, 0, proj_a):
            pass

    @pl.when(tile_in_seq == 0)
    def _():
        for buf in halo_bufs:
            buf[0, 0:HALO, :] = jnp.zeros((HALO, buf.shape[2]), F32)
        ssd_slab[0, SLAB_XBC:SLAB_XBC + SSD_XBC // 128, 0:HALO, :] = jnp.zeros((SSD_XBC // 128, HALO, 128), F32)
        state_ref[...] = jnp.zeros(state_ref.shape, F32)

    pos0 = tile_in_seq * SEQ_TILE
    _interleave(_mix_stage(0, pos0, h_ref, 0, out_ref, proj_a, prm, carry),
                proj_stage(h_ref, SUB_TILE, proj_b))
    _interleave(_mix_stage(1, pos0 + SUB_TILE, h_ref, SUB_TILE, out_ref, proj_b, prm, carry),
                proj_stage(hnext_ref, 0, proj_a))


def _ffn_kernel(h_ref, g_ref, wg_ref, wu_ref, wd_ref, fg_ref, out_ref, hn_ref, acc_ref, *, final_norm):
    h = h_ref[...]
    hn_ref[...] = (_rms_scale(h) * g_ref[...]).astype(BF16)
    acc_ref[...] = h
    for j in range(D_FF // FFN_CHUNK):
        c0 = j * FFN_CHUNK
        hn = hn_ref[...]
        gate = _dot(hn, wg_ref[:, c0:c0 + FFN_CHUNK].astype(BF16))
        up = _dot(hn, wu_ref[:, c0:c0 + FFN_CHUNK].astype(BF16))
        act = (_silu(gate) * up).astype(BF16)
        acc_ref[...] += _dot(act, wd_ref[c0:c0 + FFN_CHUNK, :].astype(BF16))
    res = acc_ref[...]
    if final_norm:
        res = _rms_scale(res) * fg_ref[...]
    out_ref[...] = res


def _layer_spec(arr, layer):
    zeros = (0,) * (arr.ndim - 1)
    return pl.BlockSpec((None,) + arr.shape[1:], lambda i: (layer,) + zeros)


def _mixer_call(h2d, seq_len, layer, params):
    t, d = h2d.shape
    n_tiles = t // SEQ_TILE
    assert seq_len % SEQ_TILE == 0 and t % seq_len == 0
    tile_spec = pl.BlockSpec((SEQ_TILE, d), lambda i: (i, 0))
    next_spec = pl.BlockSpec((SUB_TILE, d), lambda i: (jnp.minimum(i + 1, n_tiles - 1) * N_SUB, 0))
    return pl.pallas_call(
        functools.partial(_mixer_kernel, tiles_per_seq=seq_len // SEQ_TILE),
        grid=(n_tiles,),
        in_specs=[tile_spec, next_spec] + [_layer_spec(p, layer) for p in params],
        out_specs=tile_spec,
        out_shape=jax.ShapeDtypeStruct(h2d.shape, F32),
        scratch_shapes=[
            pltpu.VMEM((SUB_TILE, d), BF16),
            pltpu.VMEM((SUB_TILE, PROJ_COLS), F32),
            pltpu.VMEM((SUB_TILE, PROJ_COLS), F32),
            pltpu.VMEM((N_SUB, SUB_TILE + HALO, GROUP_W), F32),
            pltpu.VMEM((N_SUB, SUB_TILE + HALO, GROUP_W), F32),
            pltpu.VMEM((N_SUB, SUB_TILE + HALO, 128), F32),
            pltpu.VMEM((N_SUB, SUB_TILE + HALO, 128), F32),
            pltpu.VMEM((N_SUB, SUB_TILE + HALO, GROUP_W), F32),
            pltpu.VMEM((N_SUB, N_SSD_SLABS, SUB_TILE + HALO, 128), F32),
            pltpu.VMEM((N_SUB, GROUP_W // 128, SUB_TILE, 128), F32),
            pltpu.VMEM((SSD_GROUPS, 128, SSD_STATE), F32),
        ],
        compiler_params=pltpu.CompilerParams(
            dimension_semantics=("arbitrary",),
            vmem_limit_bytes=MIXER_VMEM_LIMIT_BYTES),
        name="mixer",
    )(h2d, h2d, *params)


def _ffn_call(h2d, layer, g, wg, wu, wd, fg, final_norm):
    t, d = h2d.shape
    tile_spec = pl.BlockSpec((FFN_TILE, d), lambda i: (i, 0))
    row_spec = pl.BlockSpec((1, d), lambda i: (0, 0))
    return pl.pallas_call(
        functools.partial(_ffn_kernel, final_norm=final_norm),
        grid=(t // FFN_TILE,),
        in_specs=[tile_spec, _layer_spec(g, layer), _layer_spec(wg, layer),
                  _layer_spec(wu, layer), _layer_spec(wd, layer), row_spec],
        out_specs=tile_spec,
        out_shape=jax.ShapeDtypeStruct(h2d.shape, F32),
        scratch_shapes=[pltpu.VMEM((FFN_TILE, d), BF16), pltpu.VMEM((FFN_TILE, d), F32)],
        compiler_params=pltpu.CompilerParams(
            dimension_semantics=("parallel",),
            vmem_limit_bytes=FFN_VMEM_LIMIT_BYTES),
        name="ffn",
    )(h2d, g, wg, wu, wd, fg)


def _pack_layer_vectors(norm_mix_g, pool_b, pool_scale, sgu_ln_g, sgu_ln_b, ssd_dt_bias, ssd_a_log, ssd_d,
                        ssd_norm_g, ssd_conv_b, sconv_w, ssd_conv_w):
    depth = norm_mix_g.shape[0]

    def rep(v):
        return jnp.repeat(v, GROUP_W // SSD_HEADS, axis=-1)

    def pad_to_row(v):
        return jnp.pad(v, [(0, 0)] * (v.ndim - 1) + [(0, D_MODEL - v.shape[-1])])

    rows = [
        norm_mix_g,
        jnp.concatenate([pool_b, pool_scale, sgu_ln_g, sgu_ln_b], axis=-1),
        jnp.concatenate([rep(ssd_dt_bias), rep(ssd_a_log), rep(ssd_d), ssd_norm_g], axis=-1),
        pad_to_row(ssd_conv_b),
        pad_to_row(sconv_w.reshape(depth, SCONV_WIDTH * GROUP_W)),
    ]
    table = jnp.concatenate([r[:, None, :] for r in rows] + [pad_to_row(ssd_conv_w)], axis=1)
    assert table.shape == (depth, N_VEC_ROWS, D_MODEL)
    return table.astype(F32)


def kernel(x, norm_mix_g, w_in, pool_w, pool_b, pool_scale, sgu_ln_g, sgu_ln_b, sgu_w, sgu_b, sconv_w, ssd_conv_w, ssd_conv_b, ssd_dt_bias, ssd_a_log, ssd_d, ssd_norm_g, w_out, norm_ffn_g, w_gate, w_up, w_down, final_norm_g):
    b, s, d = x.shape
    depth = w_in.shape[0]
    vecs = _pack_layer_vectors(norm_mix_g, pool_b, pool_scale, sgu_ln_g, sgu_ln_b, ssd_dt_bias, ssd_a_log,
                               ssd_d, ssd_norm_g, ssd_conv_b, sconv_w, ssd_conv_w)
    win = w_in.astype(BF16)
    wdt = jnp.repeat(w_in[:, :, C_DT:], GROUP_W // SSD_HEADS, axis=2).astype(BF16)
    poolw = jnp.einsum('lgcd,gh->lgchd', pool_w, jnp.eye(POOL_GROUPS, dtype=F32)).reshape(
        depth, GROUP_W, GROUP_W).astype(BF16)
    sgub = jnp.repeat(jnp.swapaxes(sgu_b, 1, 2), GROUP_W // SGU_HEADS, axis=2)
    wout = w_out.astype(BF16)
    fg = final_norm_g.reshape(1, d).astype(F32)
    mixer_params = (vecs, win, wdt, poolw, sgu_w, sgub, wout)

    h = x.reshape(b * s, d)
    for layer in range(depth):
        h = _mixer_call(h, s, layer, mixer_params)
        h = _ffn_call(h, layer, norm_ffn_g.reshape(depth, 1, d), w_gate, w_up, w_down, fg,
                      final_norm=(layer == depth - 1))
    return h.reshape(b, s, d)
```

```python
import functools

import jax
import jax.numpy as jnp
from jax import lax
from jax.experimental import pallas as pl
from jax.experimental.pallas import tpu as pltpu

F32 = jnp.float32
BF16 = jnp.bfloat16

D_MODEL = 1024
GROUP_W = 256
POOL_GROUPS = 4
SGU_BLOCK = 128
SGU_HEADS = 4
HALO = 8
SCONV_WIDTH = 3
SSD_CONV = 4
SSD_CHUNK = 64
SSD_STATE = 128
SSD_GROUPS = 2
SSD_HEADS = 4
SSD_XBC = 768
SLAB_XBC, SLAB_DT, SLAB_Z, N_SSD_SLABS = 0, 6, 8, 10
D_FF = 2816
EPS = 1e-6

C_POOL, C_U, C_V, C_CB, C_CC, C_CH, C_Z, C_XBC, C_DT = 0, 256, 512, 768, 1024, 1280, 1536, 1792, 2560
PROJ_COLS = C_DT + GROUP_W

V_NORM_G, V_POOL_SGU, V_SSD, V_CONV_B, V_SCONV_W, V_CONV_W = 0, 1, 2, 3, 4, 5
N_VEC_ROWS = V_CONV_W + SSD_CONV

SEQ_TILE = 1024
SUB_TILE = 512
N_SUB = SEQ_TILE // SUB_TILE
assert N_SUB == 2
FFN_TILE = 1024
FFN_CHUNK = 256
MIB = 1024 * 1024
MIXER_VMEM_LIMIT_BYTES = 56 * MIB
FFN_VMEM_LIMIT_BYTES = 60 * MIB
PROJ_PIECES_AFTER_MIX_PIECE = (2, 1, 1, 1, 1, 1, 1, 1, 1, 1, 1, 0)


def _dot(a, b):
    return jnp.dot(a, b, preferred_element_type=F32)


NEG_LOG2_E = -1.4426950408889634


def _silu(x):
    return x * (1.0 / (1.0 + jnp.exp2(x * NEG_LOG2_E)))


def _rms_scale(x):
    return x * lax.rsqrt(jnp.mean(x * x, axis=-1, keepdims=True) + EPS)


def _proj_stage(h_ref, row0, vec_ref, win_ref, wdt_ref, hn_buf, proj_buf):
    g = vec_ref[V_NORM_G:V_NORM_G + 1, :]
    hn_buf[...] = (_rms_scale(h_ref[row0:row0 + SUB_TILE, :]) * g).astype(BF16)
    yield
    for c0 in range(0, C_DT, GROUP_W):
        proj_buf[:, c0:c0 + GROUP_W] = _dot(hn_buf[...], win_ref[:, c0:c0 + GROUP_W])
        yield
    proj_buf[:, C_DT:C_DT + GROUP_W] = _dot(hn_buf[...], wdt_ref[...])
    yield


def _mix_stage(k, pos0, h_ref, row0, out_ref, proj, prm, carry):
    vec_ref, poolw_ref, sguw_ref, sgub_ref, wout_ref = prm
    x_buf, s2_buf, s4_buf, s8_buf, sc_buf, ssd_slab, y_nat, state_ref = carry
    ts = SUB_TILE
    nxt = (k + 1) % N_SUB
    rows = slice(row0, row0 + ts)
    n_chunks = ts // SSD_CHUNK

    def vec(row, slot, width=GROUP_W):
        return vec_ref[row:row + 1, slot * GROUP_W:slot * GROUP_W + width]

    def out_dot(m, y):
        return _dot(y.astype(BF16), wout_ref[m * GROUP_W:(m + 1) * GROUP_W, :])

    def with_halo(buf, val):
        buf[k, HALO:HALO + ts, :] = val
        buf[nxt, 0:HALO, :] = val[ts - HALO:ts, :]

    lane = lax.broadcasted_iota(jnp.int32, (1, 128), 1)
    lo_half = lane < 64

    half = SSD_CHUNK // 2

    def chunk_order(even, odd):
        return jnp.concatenate([part[c * half:(c + 1) * half] for c in range(n_chunks) for part in (even, odd)], axis=0)

    def slab_rows(slab, first_row):
        return ssd_slab[k, slab, pl.ds(HALO + first_row, ts // 2, stride=2), :]

    def store_slabs(slab0, c0, width):
        for j in range(width // 128):
            cols = proj[:, c0 + j * 128:c0 + (j + 1) * 128]
            ssd_slab[k, slab0 + j, HALO:HALO + ts, :] = cols
            if slab0 == SLAB_XBC:
                ssd_slab[nxt, slab0 + j, 0:HALO, :] = cols[ts - HALO:ts, :]

    def in_chunk_order(slab0, width):
        return jnp.concatenate([chunk_order(slab_rows(slab0 + j, 0), slab_rows(slab0 + j, 1))
                                for j in range(width // 128)], axis=1)

    store_slabs(SLAB_XBC, C_XBC, SSD_XBC)
    xbc = []
    for part in range(SSD_XBC // GROUP_W):
        halves = []
        for j in range(GROUP_W // 128):
            slab = SLAB_XBC + 2 * part + j
            cols = slice(slab * 128, (slab + 1) * 128)
            parity = []
            for p in range(2):
                conv = slab_rows(slab, p) * vec_ref[V_CONV_W + SSD_CONV - 1:V_CONV_W + SSD_CONV, cols] \
                    + vec_ref[V_CONV_B:V_CONV_B + 1, cols]
                for lag in range(1, SSD_CONV):
                    tap = V_CONV_W + SSD_CONV - 1 - lag
                    conv = conv + slab_rows(slab, p - lag) * vec_ref[tap:tap + 1, cols]
                parity.append(_silu(conv))
            halves.append(chunk_order(*parity))
        xbc.append(jnp.concatenate(halves, axis=1))
        yield
    xh, bmat, cmat = xbc

    store_slabs(SLAB_DT, C_DT, GROUP_W)
    store_slabs(SLAB_Z, C_Z, GROUP_W)
    dt = in_chunk_order(SLAB_DT, GROUP_W) + vec(V_SSD, 0)
    delta = jnp.maximum(dt, 0.0) + jnp.log(1.0 + jnp.exp(-jnp.abs(dt)))
    da = delta * (-jnp.exp(vec(V_SSD, 1)))
    x_dt = xh * delta

    def frame_of(pos):
        return (2 * pos) % SSD_CHUNK + pos // half

    rr = frame_of(lax.broadcasted_iota(jnp.int32, (SSD_CHUNK, 3 * SSD_CHUNK), 0))
    cc = frame_of(lax.broadcasted_iota(jnp.int32, (SSD_CHUNK, 3 * SSD_CHUNK), 1) % SSD_CHUNK)
    tri3 = jnp.where(cc <= rr, 1.0, 0.0).astype(BF16)
    da_hi = da.astype(BF16)
    rem = da - da_hi.astype(F32)
    da_mid = rem.astype(BF16)
    da_lo = (rem - da_mid.astype(F32)).astype(BF16)
    acs_chunks = []
    for c in range(n_chunks):
        crow = slice(c * SSD_CHUNK, (c + 1) * SSD_CHUNK)
        acs_chunks.append(_dot(tri3, jnp.concatenate([da_hi[crow], da_mid[crow], da_lo[crow]], axis=0)))
    yield

    v = proj[:, C_V:C_V + GROUP_W]
    mu = jnp.mean(v, axis=-1, keepdims=True)
    vc = v - mu
    var = jnp.mean(vc * vc, axis=-1, keepdims=True)
    vn = (vc * lax.rsqrt(var + EPS)) * vec(V_POOL_SGU, 2) + vec(V_POOL_SGU, 3)
    ri = lax.broadcasted_iota(jnp.int32, (SGU_BLOCK, SGU_BLOCK), 0) // SSD_CHUNK
    ci = lax.broadcasted_iota(jnp.int32, (SGU_BLOCK, SGU_BLOCK), 1) // SSD_CHUNK
    chunk_causal = ri >= ci
    n_blocks = ts // SGU_BLOCK
    head_w = GROUP_W // SGU_HEADS
    vn_t = [vn[b * SGU_BLOCK:(b + 1) * SGU_BLOCK, :].T.astype(BF16) for b in range(n_blocks)]
    head_mix_t = []
    for hd in range(SGU_HEADS):
        w_t = jnp.where(chunk_causal, sguw_ref[hd], 0.0).T.astype(BF16)
        lhs = jnp.concatenate([vn_t[b][hd * head_w:(hd + 1) * head_w, :] for b in range(n_blocks)], axis=0)
        head_mix_t.append(_dot(lhs, w_t))
    yield

    row64 = frame_of(lax.broadcasted_iota(jnp.int32, (SSD_CHUNK, 128), 0))
    col64 = frame_of(lax.broadcasted_iota(jnp.int32, (SSD_CHUNK, 128), 1) % SSD_CHUNK)
    causal2 = col64 <= row64
    blocks = [(c, g) for c in range(n_chunks) for g in range(SSD_GROUPS)]

    def blk(arr, c, g):
        return arr[c * SSD_CHUNK:(c + 1) * SSD_CHUNK, g * 128:(g + 1) * 128]

    a2, bt_bf, upd = {}, {}, {}
    for c, g in blocks:
        a2[c, g] = acs_chunks[c][:, g * 128:(g + 1) * 128]
        bt_bf[c, g] = blk(bmat, c, g).T.astype(BF16)
        last = a2[c, g][SSD_CHUNK - 1:SSD_CHUNK, :]
        upd[c, g] = _dot(bt_bf[c, g], (blk(x_dt, c, g) * jnp.exp(last - a2[c, g])).astype(BF16))
    yield

    xp = proj[:, C_POOL:C_POOL + GROUP_W]
    with_halo(x_buf, xp)
    s2 = xp + x_buf[k, HALO - 1:HALO - 1 + ts, :]
    with_halo(s2_buf, s2)
    s4 = s2 + s2_buf[k, HALO - 2:HALO - 2 + ts, :]
    with_halo(s4_buf, s4[:, 128:256])
    s8 = s4[:, 128:256] + s4_buf[k, HALO - 4:HALO - 4 + ts, :]
    with_halo(s8_buf, s8)
    s16 = s8 + s8_buf[k, 0:ts, :]
    pos1 = (lax.broadcasted_iota(jnp.int32, (ts, 128), 0) + (pos0 + 1)).astype(F32)
    cnt_a = jnp.where(lo_half, jnp.minimum(pos1, 2.0), jnp.minimum(pos1, 4.0))
    cnt_b = jnp.where(lo_half, jnp.minimum(pos1, 8.0), jnp.minimum(pos1, 16.0))
    pooled = jnp.concatenate([jnp.where(lo_half, s2[:, 0:128], s4[:, 0:128]) / cnt_a,
                              jnp.where(lo_half, s8, s16) / cnt_b], axis=1) - xp
    pool_mm = _dot(pooled.astype(BF16), poolw_ref[...])

    mixed_blocks = []
    for b in range(n_blocks):
        mixed_t = jnp.concatenate([head_mix_t[hd][b * head_w:(b + 1) * head_w, :] for hd in range(SGU_HEADS)], axis=0)
        mixed_blocks.append(mixed_t.T + sgub_ref[...])
    y_b = proj[:, C_U:C_U + GROUP_W] * jnp.concatenate(mixed_blocks, axis=0)
    d_b = out_dot(1, y_b)
    yield

    states = [state_ref[g] for g in range(SSD_GROUPS)]
    both = {}
    for c, g in blocks:
        rhs = jnp.concatenate([bt_bf[c, g], bt_bf[c, g], states[g].astype(BF16)], axis=1)
        both[c, g] = _dot(blk(cmat, c, g).astype(BF16), rhs)
        last = a2[c, g][SSD_CHUNK - 1:SSD_CHUNK, :]
        states[g] = states[g] * jnp.exp(last) + upd[c, g]
    for g in range(SSD_GROUPS):
        state_ref[g] = states[g]
    yield

    prod = proj[:, C_CC:C_CC + GROUP_W] * proj[:, C_CH:C_CH + GROUP_W]
    with_halo(sc_buf, prod)
    conv = prod * vec(V_SCONV_W, SCONV_WIDTH - 1)
    for lag in range(1, SCONV_WIDTH):
        conv = conv + sc_buf[k, HALO - lag:HALO - lag + ts, :] * vec(V_SCONV_W, SCONV_WIDTH - 1 - lag)
    y_c = proj[:, C_CB:C_CB + GROUP_W] * conv
    d_c = out_dot(2, y_c)
    y_a = (pool_mm + vec(V_POOL_SGU, 0)) * vec(V_POOL_SGU, 1)
    d_a = out_dot(0, y_a)
    out_ref[rows, :] = h_ref[rows, :] + d_b
    yield

    y_diag = {}
    for c, g in blocks:
        a_key = jnp.sum(jnp.where(row64 <= col64, blk(da, c, g), 0.0), axis=0, keepdims=True)
        decay = jnp.exp(jnp.where(causal2, a2[c, g] - a_key, -jnp.inf))
        x2 = blk(x_dt, c, g)
        x_bd = jnp.concatenate([jnp.where(lo_half, x2, 0.0), jnp.where(lo_half, 0.0, x2)], axis=0)
        y_diag[c, g] = _dot((both[c, g][:, 0:128] * decay).astype(BF16), x_bd.astype(BF16))
    out_ref[rows, :] += d_a + d_c
    yield

    y_ssd = jnp.concatenate(
        [jnp.concatenate([y_diag[c, g] + both[c, g][:, 128:256] * jnp.exp(a2[c, g]) for g in range(SSD_GROUPS)], axis=1)
         for c in range(n_chunks)], axis=0)
    y = (y_ssd + xh * vec(V_SSD, 2)) * _silu(in_chunk_order(SLAB_Z, GROUP_W))
    y_d = jnp.concatenate([_rms_scale(y[:, 0:128]), _rms_scale(y[:, 128:256])], axis=1) * vec(V_SSD, 3)
    for j in range(GROUP_W // 128):
        for c in range(n_chunks):
            for p in range(2):
                r0 = c * SSD_CHUNK + p * half
                y_nat[k, j, pl.ds(c * SSD_CHUNK + p, half, stride=2), :] = y_d[r0:r0 + half, j * 128:(j + 1) * 128]
    d_d = out_dot(3, jnp.concatenate([y_nat[k, j] for j in range(GROUP_W // 128)], axis=1))
    yield
    out_ref[rows, :] += d_d
    yield


def _interleave(main, side):
    for n_side in PROJ_PIECES_AFTER_MIX_PIECE:
        next(main)
        for _ in range(n_side):
            next(side)
    assert next(main, "done") == "done" and next(side, "done") == "done"


def _mixer_kernel(h_ref, hnext_ref, vec_ref, win_ref, wdt_ref, poolw_ref, sguw_ref, sgub_ref, wout_ref,
                  out_ref, hn_buf, proj_a, proj_b, x_buf, s2_buf, s4_buf, s8_buf, sc_buf, ssd_slab, y_nat,
                  state_ref, *, tiles_per_seq):
    prm = (vec_ref, poolw_ref, sguw_ref, sgub_ref, wout_ref)
    halo_bufs = (x_buf, s2_buf, s4_buf, s8_buf, sc_buf)
    carry = halo_bufs + (ssd_slab, y_nat, state_ref)
    step = pl.program_id(0)
    tile_in_seq = step % tiles_per_seq

    def proj_stage(src_ref, row0, dst):
        return _proj_stage(src_ref, row0, vec_ref, win_ref, wdt_ref, hn_buf, dst)

    @pl.when(step == 0)
    def _():
        for _ in proj_stage(h_ref, 0, proj_a):
            pass

    @pl.when(tile_in_seq == 0)
    def _():
        for buf in halo_bufs:
            buf[0, 0:HALO, :] = jnp.zeros((HALO, buf.shape[2]), F32)
        ssd_slab[0, SLAB_XBC:SLAB_XBC + SSD_XBC // 128, 0:HALO, :] = jnp.zeros((SSD_XBC // 128, HALO, 128), F32)
        state_ref[...] = jnp.zeros(state_ref.shape, F32)

    pos0 = tile_in_seq * SEQ_TILE
    _interleave(_mix_stage(0, pos0, h_ref, 0, out_ref, proj_a, prm, carry),
                proj_stage(h_ref, SUB_TILE, proj_b))
    _interleave(_mix_stage(1, pos0 + SUB_TILE, h_ref, SUB_TILE, out_ref, proj_b, prm, carry),
                proj_stage(hnext_ref, 0, proj_a))


def _ffn_kernel(h_ref, g_ref, wg_ref, wu_ref, wd_ref, fg_ref, out_ref, hn_ref, acc_ref, *, final_norm):
    h = h_ref[...]
    hn_ref[...] = (_rms_scale(h) * g_ref[...]).astype(BF16)
    acc_ref[...] = h
    for j in range(D_FF // FFN_CHUNK):
        c0 = j * FFN_CHUNK
        hn = hn_ref[...]
        gate = _dot(hn, wg_ref[:, c0:c0 + FFN_CHUNK].astype(BF16))
        up = _dot(hn, wu_ref[:, c0:c0 + FFN_CHUNK].astype(BF16))
        act = (_silu(gate) * up).astype(BF16)
        acc_ref[...] += _dot(act, wd_ref[c0:c0 + FFN_CHUNK, :].astype(BF16))
    res = acc_ref[...]
    if final_norm:
        res = _rms_scale(res) * fg_ref[...]
    out_ref[...] = res


def _layer_spec(arr, layer):
    zeros = (0,) * (arr.ndim - 1)
    return pl.BlockSpec((None,) + arr.shape[1:], lambda i: (layer,) + zeros)


def _mixer_call(h2d, seq_len, layer, params):
    t, d = h2d.shape
    n_tiles = t // SEQ_TILE
    assert seq_len % SEQ_TILE == 0 and t % seq_len == 0
    tile_spec = pl.BlockSpec((SEQ_TILE, d), lambda i: (i, 0))
    next_spec = pl.BlockSpec((SUB_TILE, d), lambda i: (jnp.minimum(i + 1, n_tiles - 1) * N_SUB, 0))
    return pl.pallas_call(
        functools.partial(_mixer_kernel, tiles_per_seq=seq_len // SEQ_TILE),
        grid=(n_tiles,),
        in_specs=[tile_spec, next_spec] + [_layer_spec(p, layer) for p in params],
        out_specs=tile_spec,
        out_shape=jax.ShapeDtypeStruct(h2d.shape, F32),
        scratch_shapes=[
            pltpu.VMEM((SUB_TILE, d), BF16),
            pltpu.VMEM((SUB_TILE, PROJ_COLS), F32),
            pltpu.VMEM((SUB_TILE, PROJ_COLS), F32),
            pltpu.VMEM((N_SUB, SUB_TILE + HALO, GROUP_W), F32),
            pltpu.VMEM((N_SUB, SUB_TILE + HALO, GROUP_W), F32),
            pltpu.VMEM((N_SUB, SUB_TILE + HALO, 128), F32),
            pltpu.VMEM((N_SUB, SUB_TILE + HALO, 128), F32),
            pltpu.VMEM((N_SUB, SUB_TILE + HALO, GROUP_W), F32),
            pltpu.VMEM((N_SUB, N_SSD_SLABS, SUB_TILE + HALO, 128), F32),
            pltpu.VMEM((N_SUB, GROUP_W // 128, SUB_TILE, 128), F32),
            pltpu.VMEM((SSD_GROUPS, 128, SSD_STATE), F32),
        ],
        compiler_params=pltpu.CompilerParams(
            dimension_semantics=("arbitrary",),
            vmem_limit_bytes=MIXER_VMEM_LIMIT_BYTES),
        name="mixer",
    )(h2d, h2d, *params)


def _ffn_call(h2d, layer, g, wg, wu, wd, fg, final_norm):
    t, d = h2d.shape
    tile_spec = pl.BlockSpec((FFN_TILE, d), lambda i: (i, 0))
    row_spec = pl.BlockSpec((1, d), lambda i: (0, 0))
    return pl.pallas_call(
        functools.partial(_ffn_kernel, final_norm=final_norm),
        grid=(t // FFN_TILE,),
        in_specs=[tile_spec, _layer_spec(g, layer), _layer_spec(wg, layer),
                  _layer_spec(wu, layer), _layer_spec(wd, layer), row_spec],
        out_specs=tile_spec,
        out_shape=jax.ShapeDtypeStruct(h2d.shape, F32),
        scratch_shapes=[pltpu.VMEM((FFN_TILE, d), BF16), pltpu.VMEM((FFN_TILE, d), F32)],
        compiler_params=pltpu.CompilerParams(
            dimension_semantics=("parallel",),
            vmem_limit_bytes=FFN_VMEM_LIMIT_BYTES),
        name="ffn",
    )(h2d, g, wg, wu, wd, fg)


def _pack_layer_vectors(norm_mix_g, pool_b, pool_scale, sgu_ln_g, sgu_ln_b, ssd_dt_bias, ssd_a_log, ssd_d,
                        ssd_norm_g, ssd_conv_b, sconv_w, ssd_conv_w):
    depth = norm_mix_g.shape[0]

    def rep(v):
        return jnp.repeat(v, GROUP_W // SSD_HEADS, axis=-1)

    def pad_to_row(v):
        return jnp.pad(v, [(0, 0)] * (v.ndim - 1) + [(0, D_MODEL - v.shape[-1])])

    rows = [
        norm_mix_g,
        jnp.concatenate([pool_b, pool_scale, sgu_ln_g, sgu_ln_b], axis=-1),
        jnp.concatenate([rep(ssd_dt_bias), rep(ssd_a_log), rep(ssd_d), ssd_norm_g], axis=-1),
        pad_to_row(ssd_conv_b),
        pad_to_row(sconv_w.reshape(depth, SCONV_WIDTH * GROUP_W)),
    ]
    table = jnp.concatenate([r[:, None, :] for r in rows] + [pad_to_row(ssd_conv_w)], axis=1)
    assert table.shape == (depth, N_VEC_ROWS, D_MODEL)
    return table.astype(F32)


def kernel(x, norm_mix_g, w_in, pool_w, pool_b, pool_scale, sgu_ln_g, sgu_ln_b, sgu_w, sgu_b, sconv_w, ssd_conv_w, ssd_conv_b, ssd_dt_bias, ssd_a_log, ssd_d, ssd_norm_g, w_out, norm_ffn_g, w_gate, w_up, w_down, final_norm_g):
    b, s, d = x.shape
    depth = w_in.shape[0]
    vecs = _pack_layer_vectors(norm_mix_g, pool_b, pool_scale, sgu_ln_g, sgu_ln_b, ssd_dt_bias, ssd_a_log,
                               ssd_d, ssd_norm_g, ssd_conv_b, sconv_w, ssd_conv_w)
    win = w_in.astype(BF16)
    wdt = jnp.repeat(w_in[:, :, C_DT:], GROUP_W // SSD_HEADS, axis=2).astype(BF16)
    poolw = jnp.einsum('lgcd,gh->lgchd', pool_w, jnp.eye(POOL_GROUPS, dtype=F32)).reshape(
        depth, GROUP_W, GROUP_W).astype(BF16)
    sgub = jnp.repeat(jnp.swapaxes(sgu_b, 1, 2), GROUP_W // SGU_HEADS, axis=2)
    wout = w_out.astype(BF16)
    fg = final_norm_g.reshape(1, d).astype(F32)
    mixer_params = (vecs, win, wdt, poolw, sgu_w, sgub, wout)

    h = x.reshape(b * s, d)
    for layer in range(depth):
        h = _mixer_call(h, s, layer, mixer_params)
        h = _ffn_call(h, layer, norm_ffn_g.reshape(depth, 1, d), w_gate, w_up, w_down, fg,
                      final_norm=(layer == depth - 1))
    return h.reshape(b, s, d)
```

```python
import functools

import jax
import jax.numpy as jnp
from jax import lax
from jax.experimental import pallas as pl
from jax.experimental.pallas import tpu as pltpu

F32 = jnp.float32
BF16 = jnp.bfloat16

D_MODEL = 1024
GROUP_W = 256
POOL_GROUPS = 4
SGU_BLOCK = 128
SGU_HEADS = 4
HALO = 8
SCONV_WIDTH = 3
SSD_CONV = 4
SSD_CHUNK = 64
SSD_STATE = 128
SSD_GROUPS = 2
SSD_HEADS = 4
SSD_XBC = 768
SLAB_XBC, SLAB_DT, SLAB_Z, N_SSD_SLABS = 0, 6, 8, 10
L_X, L_S2, L_S4, L_S8, L_SC, N_LVL_SLABS = 0, 2, 4, 6, 7, 9
D_FF = 2816
EPS = 1e-6

C_POOL, C_U, C_V, C_CB, C_CC, C_CH, C_Z, C_XBC, C_DT = 0, 256, 512, 768, 1024, 1280, 1536, 1792, 2560
PROJ_COLS = C_DT + GROUP_W

V_NORM_G, V_POOL_SGU, V_SSD, V_CONV_B, V_SCONV_W, V_CONV_W = 0, 1, 2, 3, 4, 5
N_VEC_ROWS = V_CONV_W + SSD_CONV

SEQ_TILE = 1024
SUB_TILE = 512
N_SUB = SEQ_TILE // SUB_TILE
assert N_SUB == 2
FFN_TILE = 1024
FFN_CHUNK = 256
MIB = 1024 * 1024
MIXER_VMEM_LIMIT_BYTES = 56 * MIB
FFN_VMEM_LIMIT_BYTES = 60 * MIB
PROJ_PIECES_AFTER_MIX_PIECE = (2, 1, 1, 1, 1, 1, 1, 1, 1, 1, 1, 0)


def _dot(a, b):
    return jnp.dot(a, b, preferred_element_type=F32)


NEG_LOG2_E = -1.4426950408889634


def _silu(x):
    return x * (1.0 / (1.0 + jnp.exp2(x * NEG_LOG2_E)))


def _rms_scale(x):
    return x * lax.rsqrt(jnp.mean(x * x, axis=-1, keepdims=True) + EPS)


def _proj_stage(h_ref, row0, vec_ref, win_ref, wdt_ref, hn_buf, proj_buf):
    g = vec_ref[V_NORM_G:V_NORM_G + 1, :]
    hn_buf[...] = (_rms_scale(h_ref[row0:row0 + SUB_TILE, :]) * g).astype(BF16)
    yield
    for c0 in range(0, C_DT, GROUP_W):
        proj_buf[:, c0:c0 + GROUP_W] = _dot(hn_buf[...], win_ref[:, c0:c0 + GROUP_W])
        yield
    proj_buf[:, C_DT:C_DT + GROUP_W] = _dot(hn_buf[...], wdt_ref[...])
    yield


def _mix_stage(k, pos0, h_ref, row0, out_ref, proj, prm, carry):
    vec_ref, poolw_ref, sguw_ref, sgub_ref, wout_ref = prm
    lvl_slab, sc_nat, ssd_slab, y_nat, state_ref = carry
    ts = SUB_TILE
    nxt = (k + 1) % N_SUB
    rows = slice(row0, row0 + ts)
    n_chunks = ts // SSD_CHUNK

    def vec(row, slot, width=GROUP_W):
        return vec_ref[row:row + 1, slot * GROUP_W:slot * GROUP_W + width]

    def out_dot(m, y):
        return _dot(y.astype(BF16), wout_ref[m * GROUP_W:(m + 1) * GROUP_W, :])

    def lvl_rows(slab, first_row):
        return lvl_slab[k, slab, pl.ds(HALO + first_row, ts // 2, stride=2), :]

    def lvl_frames(slab, first_row=0):
        return lvl_slab[k, slab, HALO + first_row:HALO + first_row + ts, :]

    def keep_frames(slab0, val):
        for j in range(val.shape[1] // 128):
            lvl_slab[k, slab0 + j, HALO:HALO + ts, :] = val[:, j * 128:(j + 1) * 128]
            lvl_slab[nxt, slab0 + j, 0:HALO, :] = val[ts - HALO:ts, j * 128:(j + 1) * 128]

    def window_level(src, dst, lag, n_slabs):
        for j in range(n_slabs):
            for p in range(2):
                lvl_slab[k, dst + j, pl.ds(HALO + p, ts // 2, stride=2), :] = (
                    lvl_rows(src + j, p) + lvl_rows(src + j, p - lag))
            lvl_slab[nxt, dst + j, 0:HALO, :] = lvl_slab[k, dst + j, ts:ts + HALO, :]

    lane = lax.broadcasted_iota(jnp.int32, (1, 128), 1)
    lo_half = lane < 64

    half = SSD_CHUNK // 2

    def chunk_order(even, odd):
        return jnp.concatenate([part[c * half:(c + 1) * half] for c in range(n_chunks) for part in (even, odd)], axis=0)

    def slab_rows(slab, first_row):
        return ssd_slab[k, slab, pl.ds(HALO + first_row, ts // 2, stride=2), :]

    def store_slabs(slab0, c0, width):
        for j in range(width // 128):
            cols = proj[:, c0 + j * 128:c0 + (j + 1) * 128]
            ssd_slab[k, slab0 + j, HALO:HALO + ts, :] = cols
            if slab0 == SLAB_XBC:
                ssd_slab[nxt, slab0 + j, 0:HALO, :] = cols[ts - HALO:ts, :]

    def in_chunk_order(slab0, width):
        return jnp.concatenate([chunk_order(slab_rows(slab0 + j, 0), slab_rows(slab0 + j, 1))
                                for j in range(width // 128)], axis=1)

    store_slabs(SLAB_XBC, C_XBC, SSD_XBC)
    xbc = []
    for part in range(SSD_XBC // GROUP_W):
        halves = []
        for j in range(GROUP_W // 128):
            slab = SLAB_XBC + 2 * part + j
            cols = slice(slab * 128, (slab + 1) * 128)
            parity = []
            for p in range(2):
                conv = slab_rows(slab, p) * vec_ref[V_CONV_W + SSD_CONV - 1:V_CONV_W + SSD_CONV, cols] \
                    + vec_ref[V_CONV_B:V_CONV_B + 1, cols]
                for lag in range(1, SSD_CONV):
                    tap = V_CONV_W + SSD_CONV - 1 - lag
                    conv = conv + slab_rows(slab, p - lag) * vec_ref[tap:tap + 1, cols]
                parity.append(_silu(conv))
            halves.append(chunk_order(*parity))
        xbc.append(jnp.concatenate(halves, axis=1))
        yield
    xh, bmat, cmat = xbc

    store_slabs(SLAB_DT, C_DT, GROUP_W)
    store_slabs(SLAB_Z, C_Z, GROUP_W)
    dt = in_chunk_order(SLAB_DT, GROUP_W) + vec(V_SSD, 0)
    delta = jnp.maximum(dt, 0.0) + jnp.log(1.0 + jnp.exp(-jnp.abs(dt)))
    da = delta * (-jnp.exp(vec(V_SSD, 1)))
    x_dt = xh * delta

    def frame_of(pos):
        return (2 * pos) % SSD_CHUNK + pos // half

    rr = frame_of(lax.broadcasted_iota(jnp.int32, (SSD_CHUNK, 3 * SSD_CHUNK), 0))
    cc = frame_of(lax.broadcasted_iota(jnp.int32, (SSD_CHUNK, 3 * SSD_CHUNK), 1) % SSD_CHUNK)
    tri3 = jnp.where(cc <= rr, 1.0, 0.0).astype(BF16)
    da_hi = da.astype(BF16)
    rem = da - da_hi.astype(F32)
    da_mid = rem.astype(BF16)
    da_lo = (rem - da_mid.astype(F32)).astype(BF16)
    acs_chunks = []
    for c in range(n_chunks):
        crow = slice(c * SSD_CHUNK, (c + 1) * SSD_CHUNK)
        acs_chunks.append(_dot(tri3, jnp.concatenate([da_hi[crow], da_mid[crow], da_lo[crow]], axis=0)))
    yield

    v = proj[:, C_V:C_V + GROUP_W]
    mu = jnp.mean(v, axis=-1, keepdims=True)
    vc = v - mu
    var = jnp.mean(vc * vc, axis=-1, keepdims=True)
    vn = ((vc * lax.rsqrt(var + EPS)) * vec(V_POOL_SGU, 2) + vec(V_POOL_SGU, 3)).astype(BF16)
    ri = lax.broadcasted_iota(jnp.int32, (SGU_BLOCK, SGU_BLOCK), 0) // SSD_CHUNK
    ci = lax.broadcasted_iota(jnp.int32, (SGU_BLOCK, SGU_BLOCK), 1) // SSD_CHUNK
    chunk_causal = ri >= ci
    lane256 = lax.broadcasted_iota(jnp.int32, (1, GROUP_W), 1)
    w_heads = [jnp.where(chunk_causal, sguw_ref[hd], 0.0).astype(BF16) for hd in range(SGU_HEADS)]
    head_mix = [[_dot(w_heads[hd], vn[blk * SGU_BLOCK:(blk + 1) * SGU_BLOCK, :]) for hd in range(SGU_HEADS)]
                for blk in range(ts // SGU_BLOCK)]
    yield

    row64 = frame_of(lax.broadcasted_iota(jnp.int32, (SSD_CHUNK, 128), 0))
    col64 = frame_of(lax.broadcasted_iota(jnp.int32, (SSD_CHUNK, 128), 1) % SSD_CHUNK)
    causal2 = col64 <= row64
    blocks = [(c, g) for c in range(n_chunks) for g in range(SSD_GROUPS)]

    def blk(arr, c, g):
        return arr[c * SSD_CHUNK:(c + 1) * SSD_CHUNK, g * 128:(g + 1) * 128]

    a2, bt_bf, upd = {}, {}, {}
    for c, g in blocks:
        a2[c, g] = acs_chunks[c][:, g * 128:(g + 1) * 128]
        bt_bf[c, g] = blk(bmat, c, g).T.astype(BF16)
        last = a2[c, g][SSD_CHUNK - 1:SSD_CHUNK, :]
        upd[c, g] = _dot(bt_bf[c, g], (blk(x_dt, c, g) * jnp.exp(last - a2[c, g])).astype(BF16))
    yield

    xp = proj[:, C_POOL:C_POOL + GROUP_W]
    keep_frames(L_X, xp)
    window_level(L_X, L_S2, 1, 2)
    window_level(L_S2, L_S4, 2, 2)
    window_level(L_S4 + 1, L_S8, 4, 1)
    s8 = lvl_frames(L_S8)
    s16 = s8 + lvl_frames(L_S8, -8)
    pos1 = (lax.broadcasted_iota(jnp.int32, (ts, 128), 0) + (pos0 + 1)).astype(F32)
    cnt_a = jnp.where(lo_half, jnp.minimum(pos1, 2.0), jnp.minimum(pos1, 4.0))
    cnt_b = jnp.where(lo_half, jnp.minimum(pos1, 8.0), jnp.minimum(pos1, 16.0))
    pooled = jnp.concatenate([jnp.where(lo_half, lvl_frames(L_S2), lvl_frames(L_S4)) / cnt_a,
                              jnp.where(lo_half, s8, s16) / cnt_b], axis=1) - xp
    pool_mm = _dot(pooled.astype(BF16), poolw_ref[...])

    mixed_blocks = []
    for per_head in head_mix:
        mixed = per_head[0]
        for hd in range(1, SGU_HEADS):
            mixed = jnp.where(lane256 >= hd * 64, per_head[hd], mixed)
        mixed_blocks.append(mixed + sgub_ref[...])
    y_b = proj[:, C_U:C_U + GROUP_W] * jnp.concatenate(mixed_blocks, axis=0)
    d_b = out_dot(1, y_b)
    yield

    states = [state_ref[g] for g in range(SSD_GROUPS)]
    both = {}
    for c, g in blocks:
        rhs = jnp.concatenate([bt_bf[c, g], bt_bf[c, g], states[g].astype(BF16)], axis=1)
        both[c, g] = _dot(blk(cmat, c, g).astype(BF16), rhs)
        last = a2[c, g][SSD_CHUNK - 1:SSD_CHUNK, :]
        states[g] = states[g] * jnp.exp(last) + upd[c, g]
    for g in range(SSD_GROUPS):
        state_ref[g] = states[g]
    yield

    prod = proj[:, C_CC:C_CC + GROUP_W] * proj[:, C_CH:C_CH + GROUP_W]
    keep_frames(L_SC, prod)
    for j in range(GROUP_W // 128):
        for p in range(2):
            conv = None
            for lag in range(SCONV_WIDTH):
                term = lvl_rows(L_SC + j, p - lag) * vec(V_SCONV_W, SCONV_WIDTH - 1 - lag)[:, j * 128:(j + 1) * 128]
                conv = term if conv is None else conv + term
            sc_nat[k, j, pl.ds(p, ts // 2, stride=2), :] = conv
    y_c = proj[:, C_CB:C_CB + GROUP_W] * jnp.concatenate([sc_nat[k, j] for j in range(GROUP_W // 128)], axis=1)
    d_c = out_dot(2, y_c)
    y_a = (pool_mm + vec(V_POOL_SGU, 0)) * vec(V_POOL_SGU, 1)
    d_a = out_dot(0, y_a)
    out_ref[rows, :] = h_ref[rows, :] + d_b
    yield

    y_diag = {}
    for c, g in blocks:
        a_key = jnp.sum(jnp.where(row64 <= col64, blk(da, c, g), 0.0), axis=0, keepdims=True)
        decay = jnp.exp(jnp.where(causal2, a2[c, g] - a_key, -jnp.inf))
        x2 = blk(x_dt, c, g)
        x_bd = jnp.concatenate([jnp.where(lo_half, x2, 0.0), jnp.where(lo_half, 0.0, x2)], axis=0)
        y_diag[c, g] = _dot((both[c, g][:, 0:128] * decay).astype(BF16), x_bd.astype(BF16))
    out_ref[rows, :] += d_a + d_c
    yield

    y_ssd = jnp.concatenate(
        [jnp.concatenate([y_diag[c, g] + both[c, g][:, 128:256] * jnp.exp(a2[c, g]) for g in range(SSD_GROUPS)], axis=1)
         for c in range(n_chunks)], axis=0)
    y = (y_ssd + xh * vec(V_SSD, 2)) * _silu(in_chunk_order(SLAB_Z, GROUP_W))
    y_d = jnp.concatenate([_rms_scale(y[:, 0:128]), _rms_scale(y[:, 128:256])], axis=1) * vec(V_SSD, 3)
    for j in range(GROUP_W // 128):
        for c in range(n_chunks):
            for p in range(2):
                r0 = c * SSD_CHUNK + p * half
                y_nat[k, j, pl.ds(c * SSD_CHUNK + p, half, stride=2), :] = y_d[r0:r0 + half, j * 128:(j + 1) * 128]
    d_d = out_dot(3, jnp.concatenate([y_nat[k, j] for j in range(GROUP_W // 128)], axis=1))
    yield
    out_ref[rows, :] += d_d
    yield


def _interleave(main, side):
    for n_side in PROJ_PIECES_AFTER_MIX_PIECE:
        next(main)
        for _ in range(n_side):
            next(side)
    assert next(main, "done") == "done" and next(side, "done") == "done"


def _mixer_kernel(h_ref, hnext_ref, vec_ref, win_ref, wdt_ref, poolw_ref, sguw_ref, sgub_ref, wout_ref,
                  out_ref, hn_buf, proj_a, proj_b, lvl_slab, sc_nat, ssd_slab, y_nat, state_ref, *, tiles_per_seq):
    prm = (vec_ref, poolw_ref, sguw_ref, sgub_ref, wout_ref)
    carry = (lvl_slab, sc_nat, ssd_slab, y_nat, state_ref)
    step = pl.program_id(0)
    tile_in_seq = step % tiles_per_seq

    def proj_stage(src_ref, row0, dst):
        return _proj_stage(src_ref, row0, vec_ref, win_ref, wdt_ref, hn_buf, dst)

    @pl.when(step == 0)
    def _():
        for _ in proj_stage(h_ref, 0, proj_a):
            pass

    @pl.when(tile_in_seq == 0)
    def _():
        lvl_slab[0, :, 0:HALO, :] = jnp.zeros((N_LVL_SLABS, HALO, 128), F32)
        ssd_slab[0, SLAB_XBC:SLAB_XBC + SSD_XBC // 128, 0:HALO, :] = jnp.zeros((SSD_XBC // 128, HALO, 128), F32)
        state_ref[...] = jnp.zeros(state_ref.shape, F32)

    pos0 = tile_in_seq * SEQ_TILE
    _interleave(_mix_stage(0, pos0, h_ref, 0, out_ref, proj_a, prm, carry),
                proj_stage(h_ref, SUB_TILE, proj_b))
    _interleave(_mix_stage(1, pos0 + SUB_TILE, h_ref, SUB_TILE, out_ref, proj_b, prm, carry),
                proj_stage(hnext_ref, 0, proj_a))


def _ffn_kernel(h_ref, g_ref, wg_ref, wu_ref, wd_ref, fg_ref, out_ref, hn_ref, acc_ref, *, final_norm):
    h = h_ref[...]
    hn_ref[...] = (_rms_scale(h) * g_ref[...]).astype(BF16)
    acc_ref[...] = h
    for j in range(D_FF // FFN_CHUNK):
        c0 = j * FFN_CHUNK
        hn = hn_ref[...]
        gate = _dot(hn, wg_ref[:, c0:c0 + FFN_CHUNK].astype(BF16))
        up = _dot(hn, wu_ref[:, c0:c0 + FFN_CHUNK].astype(BF16))
        act = (_silu(gate) * up).astype(BF16)
        acc_ref[...] += _dot(act, wd_ref[c0:c0 + FFN_CHUNK, :].astype(BF16))
    res = acc_ref[...]
    if final_norm:
        res = _rms_scale(res) * fg_ref[...]
    out_ref[...] = res


def _layer_spec(arr, layer):
    zeros = (0,) * (arr.ndim - 1)
    return pl.BlockSpec((None,) + arr.shape[1:], lambda i: (layer,) + zeros)


def _mixer_call(h2d, seq_len, layer, params):
    t, d = h2d.shape
    n_tiles = t // SEQ_TILE
    assert seq_len % SEQ_TILE == 0 and t % seq_len == 0
    tile_spec = pl.BlockSpec((SEQ_TILE, d), lambda i: (i, 0))
    next_spec = pl.BlockSpec((SUB_TILE, d), lambda i: (jnp.minimum(i + 1, n_tiles - 1) * N_SUB, 0))
    return pl.pallas_call(
        functools.partial(_mixer_kernel, tiles_per_seq=seq_len // SEQ_TILE),
        grid=(n_tiles,),
        in_specs=[tile_spec, next_spec] + [_layer_spec(p, layer) for p in params],
        out_specs=tile_spec,
        out_shape=jax.ShapeDtypeStruct(h2d.shape, F32),
        scratch_shapes=[
            pltpu.VMEM((SUB_TILE, d), BF16),
            pltpu.VMEM((SUB_TILE, PROJ_COLS), F32),
            pltpu.VMEM((SUB_TILE, PROJ_COLS), F32),
            pltpu.VMEM((N_SUB, N_LVL_SLABS, SUB_TILE + HALO, 128), F32),
            pltpu.VMEM((N_SUB, GROUP_W // 128, SUB_TILE, 128), F32),
            pltpu.VMEM((N_SUB, N_SSD_SLABS, SUB_TILE + HALO, 128), F32),
            pltpu.VMEM((N_SUB, GROUP_W // 128, SUB_TILE, 128), F32),
            pltpu.VMEM((SSD_GROUPS, 128, SSD_STATE), F32),
        ],
        compiler_params=pltpu.CompilerParams(
            dimension_semantics=("arbitrary",),
            vmem_limit_bytes=MIXER_VMEM_LIMIT_BYTES),
        name="mixer",
    )(h2d, h2d, *params)


def _ffn_call(h2d, layer, g, wg, wu, wd, fg, final_norm):
    t, d = h2d.shape
    tile_spec = pl.BlockSpec((FFN_TILE, d), lambda i: (i, 0))
    row_spec = pl.BlockSpec((1, d), lambda i: (0, 0))
    return pl.pallas_call(
        functools.partial(_ffn_kernel, final_norm=final_norm),
        grid=(t // FFN_TILE,),
        in_specs=[tile_spec, _layer_spec(g, layer), _layer_spec(wg, layer),
                  _layer_spec(wu, layer), _layer_spec(wd, layer), row_spec],
        out_specs=tile_spec,
        out_shape=jax.ShapeDtypeStruct(h2d.shape, F32),
        scratch_shapes=[pltpu.VMEM((FFN_TILE, d), BF16), pltpu.VMEM((FFN_TILE, d), F32)],
        compiler_params=pltpu.CompilerParams(
            dimension_semantics=("parallel",),
            vmem_limit_bytes=FFN_VMEM_LIMIT_BYTES),
        name="ffn",
    )(h2d, g, wg, wu, wd, fg)


def _pack_layer_vectors(norm_mix_g, pool_b, pool_scale, sgu_ln_g, sgu_ln_b, ssd_dt_bias, ssd_a_log, ssd_d,
                        ssd_norm_g, ssd_conv_b, sconv_w, ssd_conv_w):
    depth = norm_mix_g.shape[0]

    def rep(v):
        return jnp.repeat(v, GROUP_W // SSD_HEADS, axis=-1)

    def pad_to_row(v):
        return jnp.pad(v, [(0, 0)] * (v.ndim - 1) + [(0, D_MODEL - v.shape[-1])])

    rows = [
        norm_mix_g,
        jnp.concatenate([pool_b, pool_scale, sgu_ln_g, sgu_ln_b], axis=-1),
        jnp.concatenate([rep(ssd_dt_bias), rep(ssd_a_log), rep(ssd_d), ssd_norm_g], axis=-1),
        pad_to_row(ssd_conv_b),
        pad_to_row(sconv_w.reshape(depth, SCONV_WIDTH * GROUP_W)),
    ]
    table = jnp.concatenate([r[:, None, :] for r in rows] + [pad_to_row(ssd_conv_w)], axis=1)
    assert table.shape == (depth, N_VEC_ROWS, D_MODEL)
    return table.astype(F32)


def kernel(x, norm_mix_g, w_in, pool_w, pool_b, pool_scale, sgu_ln_g, sgu_ln_b, sgu_w, sgu_b, sconv_w, ssd_conv_w, ssd_conv_b, ssd_dt_bias, ssd_a_log, ssd_d, ssd_norm_g, w_out, norm_ffn_g, w_gate, w_up, w_down, final_norm_g):
    b, s, d = x.shape
    depth = w_in.shape[0]
    vecs = _pack_layer_vectors(norm_mix_g, pool_b, pool_scale, sgu_ln_g, sgu_ln_b, ssd_dt_bias, ssd_a_log,
                               ssd_d, ssd_norm_g, ssd_conv_b, sconv_w, ssd_conv_w)
    win = w_in.astype(BF16)
    wdt = jnp.repeat(w_in[:, :, C_DT:], GROUP_W // SSD_HEADS, axis=2).astype(BF16)
    poolw = jnp.einsum('lgcd,gh->lgchd', pool_w, jnp.eye(POOL_GROUPS, dtype=F32)).reshape(
        depth, GROUP_W, GROUP_W).astype(BF16)
    sgub = jnp.repeat(jnp.swapaxes(sgu_b, 1, 2), GROUP_W // SGU_HEADS, axis=2)
    wout = w_out.astype(BF16)
    fg = final_norm_g.reshape(1, d).astype(F32)
    mixer_params = (vecs, win, wdt, poolw, sgu_w, sgub, wout)

    h = x.reshape(b * s, d)
    for layer in range(depth):
        h = _mixer_call(h, s, layer, mixer_params)
        h = _ffn_call(h, layer, norm_ffn_g.reshape(depth, 1, d), w_gate, w_up, w_down, fg,
                      final_norm=(layer == depth - 1))
    return h.reshape(b, s, d)
```

```python
import functools

import jax
import jax.numpy as jnp
from jax import lax
from jax.experimental import pallas as pl
from jax.experimental.pallas import tpu as pltpu

F32 = jnp.float32
BF16 = jnp.bfloat16

D_MODEL = 1024
GROUP_W = 256
POOL_GROUPS = 4
SGU_BLOCK = 128
SGU_HEADS = 4
HALO = 8
SCONV_WIDTH = 3
SSD_CONV = 4
SSD_CHUNK = 64
SSD_STATE = 128
SSD_GROUPS = 2
SSD_HEADS = 4
SSD_XBC = 768
SLAB_XBC, SLAB_DT, SLAB_Z, N_SSD_SLABS = 0, 6, 8, 10
D_FF = 2816
EPS = 1e-6

C_POOL, C_U, C_V, C_CB, C_CC, C_CH, C_Z, C_XBC, C_DT = 0, 256, 512, 768, 1024, 1280, 1536, 1792, 2560
PROJ_COLS = C_DT + GROUP_W

V_NORM_G, V_POOL_SGU, V_SSD, V_CONV_B, V_SCONV_W, V_CONV_W = 0, 1, 2, 3, 4, 5
N_VEC_ROWS = V_CONV_W + SSD_CONV

SEQ_TILE = 1024
SUB_TILE = 512
N_SUB = SEQ_TILE // SUB_TILE
assert N_SUB == 2
FFN_TILE = 1024
FFN_CHUNK = 256
MIB = 1024 * 1024
MIXER_VMEM_LIMIT_BYTES = 56 * MIB
FFN_VMEM_LIMIT_BYTES = 60 * MIB
PROJ_PIECES_AFTER_MIX_PIECE = (2, 1, 1, 1, 1, 1, 1, 1, 1, 1, 1, 0)


def _dot(a, b):
    return jnp.dot(a, b, preferred_element_type=F32)


NEG_LOG2_E = -1.4426950408889634


def _silu(x):
    return x * (1.0 / (1.0 + jnp.exp2(x * NEG_LOG2_E)))


def _rms_scale(x):
    return x * lax.rsqrt(jnp.mean(x * x, axis=-1, keepdims=True) + EPS)


def _proj_stage(h_ref, row0, vec_ref, win_ref, wdt_ref, hn_buf, proj_buf):
    g = vec_ref[V_NORM_G:V_NORM_G + 1, :]
    hn_buf[...] = (_rms_scale(h_ref[row0:row0 + SUB_TILE, :]) * g).astype(BF16)
    yield
    for c0 in range(0, C_DT, GROUP_W):
        proj_buf[:, c0:c0 + GROUP_W] = _dot(hn_buf[...], win_ref[:, c0:c0 + GROUP_W])
        yield
    proj_buf[:, C_DT:C_DT + GROUP_W] = _dot(hn_buf[...], wdt_ref[...])
    yield


def _mix_stage(k, pos0, h_ref, row0, out_ref, proj, prm, carry):
    vec_ref, poolw_ref, sguw_ref, sgub_ref, wout_ref = prm
    x_buf, s2_buf, s4_buf, s8_buf, sc_buf, ssd_slab, y_nat, state_ref = carry
    ts = SUB_TILE
    nxt = (k + 1) % N_SUB
    rows = slice(row0, row0 + ts)
    n_chunks = ts // SSD_CHUNK

    def vec(row, slot, width=GROUP_W):
        return vec_ref[row:row + 1, slot * GROUP_W:slot * GROUP_W + width]

    def out_dot(m, y):
        return _dot(y.astype(BF16), wout_ref[m * GROUP_W:(m + 1) * GROUP_W, :])

    def with_halo(buf, val):
        buf[k, HALO:HALO + ts, :] = val
        buf[nxt, 0:HALO, :] = val[ts - HALO:ts, :]

    lane = lax.broadcasted_iota(jnp.int32, (1, 128), 1)
    lo_half = lane < 64

    half = SSD_CHUNK // 2

    def chunk_order(even, odd):
        return jnp.concatenate([part[c * half:(c + 1) * half] for c in range(n_chunks) for part in (even, odd)], axis=0)

    def slab_rows(slab, first_row):
        return ssd_slab[k, slab, pl.ds(HALO + first_row, ts // 2, stride=2), :]

    def store_slabs(slab0, c0, width):
        for j in range(width // 128):
            cols = proj[:, c0 + j * 128:c0 + (j + 1) * 128]
            ssd_slab[k, slab0 + j, HALO:HALO + ts, :] = cols
            if slab0 == SLAB_XBC:
                ssd_slab[nxt, slab0 + j, 0:HALO, :] = cols[ts - HALO:ts, :]

    def in_chunk_order(slab0, width):
        return jnp.concatenate([chunk_order(slab_rows(slab0 + j, 0), slab_rows(slab0 + j, 1))
                                for j in range(width // 128)], axis=1)

    store_slabs(SLAB_XBC, C_XBC, SSD_XBC)
    xbc = []
    for part in range(SSD_XBC // GROUP_W):
        halves = []
        for j in range(GROUP_W // 128):
            slab = SLAB_XBC + 2 * part + j
            cols = slice(slab * 128, (slab + 1) * 128)
            parity = []
            for p in range(2):
                conv = slab_rows(slab, p) * vec_ref[V_CONV_W + SSD_CONV - 1:V_CONV_W + SSD_CONV, cols] \
                    + vec_ref[V_CONV_B:V_CONV_B + 1, cols]
                for lag in range(1, SSD_CONV):
                    tap = V_CONV_W + SSD_CONV - 1 - lag
                    conv = conv + slab_rows(slab, p - lag) * vec_ref[tap:tap + 1, cols]
                parity.append(_silu(conv))
            halves.append(chunk_order(*parity))
        xbc.append(jnp.concatenate(halves, axis=1))
        yield
    xh, bmat, cmat = xbc

    store_slabs(SLAB_DT, C_DT, GROUP_W)
    store_slabs(SLAB_Z, C_Z, GROUP_W)
    dt = in_chunk_order(SLAB_DT, GROUP_W) + vec(V_SSD, 0)
    delta = jnp.maximum(dt, 0.0) + jnp.log(1.0 + jnp.exp(-jnp.abs(dt)))
    da = delta * (-jnp.exp(vec(V_SSD, 1)))
    x_dt = xh * delta

    def frame_of(pos):
        return (2 * pos) % SSD_CHUNK + pos // half

    rr = frame_of(lax.broadcasted_iota(jnp.int32, (SSD_CHUNK, 3 * SSD_CHUNK), 0))
    cc = frame_of(lax.broadcasted_iota(jnp.int32, (SSD_CHUNK, 3 * SSD_CHUNK), 1) % SSD_CHUNK)
    tri3 = jnp.where(cc <= rr, 1.0, 0.0).astype(BF16)
    da_hi = da.astype(BF16)
    rem = da - da_hi.astype(F32)
    da_mid = rem.astype(BF16)
    da_lo = (rem - da_mid.astype(F32)).astype(BF16)
    acs_chunks = []
    for c in range(n_chunks):
        crow = slice(c * SSD_CHUNK, (c + 1) * SSD_CHUNK)
        acs_chunks.append(_dot(tri3, jnp.concatenate([da_hi[crow], da_mid[crow], da_lo[crow]], axis=0)))
    yield

    v = proj[:, C_V:C_V + GROUP_W]
    mu = jnp.mean(v, axis=-1, keepdims=True)
    vc = v - mu
    var = jnp.mean(vc * vc, axis=-1, keepdims=True)
    vn = ((vc * lax.rsqrt(var + EPS)) * vec(V_POOL_SGU, 2) + vec(V_POOL_SGU, 3)).astype(BF16)
    ri = lax.broadcasted_iota(jnp.int32, (SGU_BLOCK, SGU_BLOCK), 0) // SSD_CHUNK
    ci = lax.broadcasted_iota(jnp.int32, (SGU_BLOCK, SGU_BLOCK), 1) // SSD_CHUNK
    chunk_causal = ri >= ci
    lane256 = lax.broadcasted_iota(jnp.int32, (1, GROUP_W), 1)
    w_heads = [jnp.where(chunk_causal, sguw_ref[hd], 0.0).astype(BF16) for hd in range(SGU_HEADS)]
    head_mix = [[_dot(w_heads[hd], vn[blk * SGU_BLOCK:(blk + 1) * SGU_BLOCK, :]) for hd in range(SGU_HEADS)]
                for blk in range(ts // SGU_BLOCK)]
    yield

    row64 = frame_of(lax.broadcasted_iota(jnp.int32, (SSD_CHUNK, 128), 0))
    col64 = frame_of(lax.broadcasted_iota(jnp.int32, (SSD_CHUNK, 128), 1) % SSD_CHUNK)
    causal2 = col64 <= row64
    blocks = [(c, g) for c in range(n_chunks) for g in range(SSD_GROUPS)]

    def blk(arr, c, g):
        return arr[c * SSD_CHUNK:(c + 1) * SSD_CHUNK, g * 128:(g + 1) * 128]

    a2, bt_bf, upd = {}, {}, {}
    for c, g in blocks:
        a2[c, g] = acs_chunks[c][:, g * 128:(g + 1) * 128]
        bt_bf[c, g] = blk(bmat, c, g).T.astype(BF16)
        last = a2[c, g][SSD_CHUNK - 1:SSD_CHUNK, :]
        upd[c, g] = _dot(bt_bf[c, g], (blk(x_dt, c, g) * jnp.exp(last - a2[c, g])).astype(BF16))
    yield

    xp = proj[:, C_POOL:C_POOL + GROUP_W]
    with_halo(x_buf, xp)
    s2 = xp + x_buf[k, HALO - 1:HALO - 1 + ts, :]
    with_halo(s2_buf, s2)
    s4 = s2 + s2_buf[k, HALO - 2:HALO - 2 + ts, :]
    with_halo(s4_buf, s4[:, 128:256])
    s8 = s4[:, 128:256] + s4_buf[k, HALO - 4:HALO - 4 + ts, :]
    with_halo(s8_buf, s8)
    s16 = s8 + s8_buf[k, 0:ts, :]
    pos1 = (lax.broadcasted_iota(jnp.int32, (ts, 128), 0) + (pos0 + 1)).astype(F32)
    cnt_a = jnp.where(lo_half, jnp.minimum(pos1, 2.0), jnp.minimum(pos1, 4.0))
    cnt_b = jnp.where(lo_half, jnp.minimum(pos1, 8.0), jnp.minimum(pos1, 16.0))
    pooled = jnp.concatenate([jnp.where(lo_half, s2[:, 0:128], s4[:, 0:128]) / cnt_a,
                              jnp.where(lo_half, s8, s16) / cnt_b], axis=1) - xp
    pool_mm = _dot(pooled.astype(BF16), poolw_ref[...])

    mixed_blocks = []
    for per_head in head_mix:
        mixed = per_head[0]
        for hd in range(1, SGU_HEADS):
            mixed = jnp.where(lane256 >= hd * 64, per_head[hd], mixed)
        mixed_blocks.append(mixed + sgub_ref[...])
    y_b = proj[:, C_U:C_U + GROUP_W] * jnp.concatenate(mixed_blocks, axis=0)
    d_b = out_dot(1, y_b)
    yield

    states = [state_ref[g] for g in range(SSD_GROUPS)]
    both = {}
    for c, g in blocks:
        rhs = jnp.concatenate([bt_bf[c, g], bt_bf[c, g], states[g].astype(BF16)], axis=1)
        both[c, g] = _dot(blk(cmat, c, g).astype(BF16), rhs)
        last = a2[c, g][SSD_CHUNK - 1:SSD_CHUNK, :]
        states[g] = states[g] * jnp.exp(last) + upd[c, g]
    for g in range(SSD_GROUPS):
        state_ref[g] = states[g]
    yield

    prod = proj[:, C_CC:C_CC + GROUP_W] * proj[:, C_CH:C_CH + GROUP_W]
    with_halo(sc_buf, prod)
    conv = prod * vec(V_SCONV_W, SCONV_WIDTH - 1)
    for lag in range(1, SCONV_WIDTH):
        conv = conv + sc_buf[k, HALO - lag:HALO - lag + ts, :] * vec(V_SCONV_W, SCONV_WIDTH - 1 - lag)
    y_c = proj[:, C_CB:C_CB + GROUP_W] * conv
    d_c = out_dot(2, y_c)
    y_a = (pool_mm + vec(V_POOL_SGU, 0)) * vec(V_POOL_SGU, 1)
    d_a = out_dot(0, y_a)
    out_ref[rows, :] = h_ref[rows, :] + d_b
    yield

    y_diag = {}
    for c, g in blocks:
        a_key = jnp.sum(jnp.where(row64 <= col64, blk(da, c, g), 0.0), axis=0, keepdims=True)
        decay = jnp.exp(jnp.where(causal2, a2[c, g] - a_key, -jnp.inf))
        x2 = blk(x_dt, c, g)
        x_bd = jnp.concatenate([jnp.where(lo_half, x2, 0.0), jnp.where(lo_half, 0.0, x2)], axis=0)
        y_diag[c, g] = _dot((both[c, g][:, 0:128] * decay).astype(BF16), x_bd.astype(BF16))
    out_ref[rows, :] += d_a + d_c
    yield

    y_ssd = jnp.concatenate(
        [jnp.concatenate([y_diag[c, g] + both[c, g][:, 128:256] * jnp.exp(a2[c, g]) for g in range(SSD_GROUPS)], axis=1)
         for c in range(n_chunks)], axis=0)
    y = (y_ssd + xh * vec(V_SSD, 2)) * _silu(in_chunk_order(SLAB_Z, GROUP_W))
    y_d = jnp.concatenate([_rms_scale(y[:, 0:128]), _rms_scale(y[:, 128:256])], axis=1) * vec(V_SSD, 3)
    for j in range(GROUP_W // 128):
        for c in range(n_chunks):
            for p in range(2):
                r0 = c * SSD_CHUNK + p * half
                y_nat[k, j, pl.ds(c * SSD_CHUNK + p, half, stride=2), :] = y_d[r0:r0 + half, j * 128:(j + 1) * 128]
    d_d = out_dot(3, jnp.concatenate([y_nat[k, j] for j in range(GROUP_W // 128)], axis=1))
    yield
    out_ref[rows, :] += d_d
    yield


def _interleave(main, side):
    for n_side in PROJ_PIECES_AFTER_MIX_PIECE:
        next(main)
        for _ in range(n_side):
            next(side)
    assert next(main, "done") == "done" and next(side, "done") == "done"


def _mixer_kernel(h_ref, hnext_ref, vec_ref, win_ref, wdt_ref, poolw_ref, sguw_ref, sgub_ref, wout_ref,
                  out_ref, hn_buf, proj_a, proj_b, x_buf, s2_buf, s4_buf, s8_buf, sc_buf, ssd_slab, y_nat,
                  state_ref, *, tiles_per_seq):
    prm = (vec_ref, poolw_ref, sguw_ref, sgub_ref, wout_ref)
    halo_bufs = (x_buf, s2_buf, s4_buf, s8_buf, sc_buf)
    carry = halo_bufs + (ssd_slab, y_nat, state_ref)
    step = pl.program_id(0)
    tile_in_seq = step % tiles_per_seq

    def proj_stage(src_ref, row0, dst):
        return _proj_stage(src_ref, row0, vec_ref, win_ref, wdt_ref, hn_buf, dst)

    @pl.when(step == 0)
    def _():
        for _ in proj_stage(h_ref, 0, proj_a):
            pass

    @pl.when(tile_in_seq == 0)
    def _():
        for buf in halo_bufs:
            buf[0, 0:HALO, :] = jnp.zeros((HALO, buf.shape[2]), F32)
        ssd_slab[0, SLAB_XBC:SLAB_XBC + SSD_XBC // 128, 0:HALO, :] = jnp.zeros((SSD_XBC // 128, HALO, 128), F32)
        state_ref[...] = jnp.zeros(state_ref.shape, F32)

    pos0 = tile_in_seq * SEQ_TILE
    _interleave(_mix_stage(0, pos0, h_ref, 0, out_ref, proj_a, prm, carry),
                proj_stage(h_ref, SUB_TILE, proj_b))
    _interleave(_mix_stage(1, pos0 + SUB_TILE, h_ref, SUB_TILE, out_ref, proj_b, prm, carry),
                proj_stage(hnext_ref, 0, proj_a))


def _ffn_kernel(h_ref, g_ref, wg_ref, wu_ref, wd_ref, fg_ref, out_ref, hn_ref, acc_ref, *, final_norm):
    h = h_ref[...]
    hn_ref[...] = (_rms_scale(h) * g_ref[...]).astype(BF16)
    acc_ref[...] = h
    pending = None
    for j in range(D_FF // FFN_CHUNK):
        c0 = j * FFN_CHUNK
        hn = hn_ref[...]
        gate = _dot(hn, wg_ref[:, c0:c0 + FFN_CHUNK].astype(BF16))
        up = _dot(hn, wu_ref[:, c0:c0 + FFN_CHUNK].astype(BF16))
        if pending is not None:
            acc_ref[...] += _dot(pending[0], wd_ref[pending[1]:pending[1] + FFN_CHUNK, :].astype(BF16))
        pending = ((_silu(gate) * up).astype(BF16), c0)
    acc_ref[...] += _dot(pending[0], wd_ref[pending[1]:pending[1] + FFN_CHUNK, :].astype(BF16))
    res = acc_ref[...]
    if final_norm:
        res = _rms_scale(res) * fg_ref[...]
    out_ref[...] = res


def _layer_spec(arr, layer):
    zeros = (0,) * (arr.ndim - 1)
    return pl.BlockSpec((None,) + arr.shape[1:], lambda i: (layer,) + zeros)


def _mixer_call(h2d, seq_len, layer, params):
    t, d = h2d.shape
    n_tiles = t // SEQ_TILE
    assert seq_len % SEQ_TILE == 0 and t % seq_len == 0
    tile_spec = pl.BlockSpec((SEQ_TILE, d), lambda i: (i, 0))
    next_spec = pl.BlockSpec((SUB_TILE, d), lambda i: (jnp.minimum(i + 1, n_tiles - 1) * N_SUB, 0))
    return pl.pallas_call(
        functools.partial(_mixer_kernel, tiles_per_seq=seq_len // SEQ_TILE),
        grid=(n_tiles,),
        in_specs=[tile_spec, next_spec] + [_layer_spec(p, layer) for p in params],
        out_specs=tile_spec,
        out_shape=jax.ShapeDtypeStruct(h2d.shape, F32),
        scratch_shapes=[
            pltpu.VMEM((SUB_TILE, d), BF16),
            pltpu.VMEM((SUB_TILE, PROJ_COLS), F32),
            pltpu.VMEM((SUB_TILE, PROJ_COLS), F32),
            pltpu.VMEM((N_SUB, SUB_TILE + HALO, GROUP_W), F32),
            pltpu.VMEM((N_SUB, SUB_TILE + HALO, GROUP_W), F32),
            pltpu.VMEM((N_SUB, SUB_TILE + HALO, 128), F32),
            pltpu.VMEM((N_SUB, SUB_TILE + HALO, 128), F32),
            pltpu.VMEM((N_SUB, SUB_TILE + HALO, GROUP_W), F32),
            pltpu.VMEM((N_SUB, N_SSD_SLABS, SUB_TILE + HALO, 128), F32),
            pltpu.VMEM((N_SUB, GROUP_W // 128, SUB_TILE, 128), F32),
            pltpu.VMEM((SSD_GROUPS, 128, SSD_STATE), F32),
        ],
        compiler_params=pltpu.CompilerParams(
            dimension_semantics=("arbitrary",),
            vmem_limit_bytes=MIXER_VMEM_LIMIT_BYTES),
        name="mixer",
    )(h2d, h2d, *params)


def _ffn_call(h2d, layer, g, wg, wu, wd, fg, final_norm):
    t, d = h2d.shape
    tile_spec = pl.BlockSpec((FFN_TILE, d), lambda i: (i, 0))
    row_spec = pl.BlockSpec((1, d), lambda i: (0, 0))
    return pl.pallas_call(
        functools.partial(_ffn_kernel, final_norm=final_norm),
        grid=(t // FFN_TILE,),
        in_specs=[tile_spec, _layer_spec(g, layer), _layer_spec(wg, layer),
                  _layer_spec(wu, layer), _layer_spec(wd, layer), row_spec],
        out_specs=tile_spec,
        out_shape=jax.ShapeDtypeStruct(h2d.shape, F32),
        scratch_shapes=[pltpu.VMEM((FFN_TILE, d), BF16), pltpu.VMEM((FFN_TILE, d), F32)],
        compiler_params=pltpu.CompilerParams(
            dimension_semantics=("parallel",),
            vmem_limit_bytes=FFN_VMEM_LIMIT_BYTES),
        name="ffn",
    )(h2d, g, wg, wu, wd, fg)


def _pack_layer_vectors(norm_mix_g, pool_b, pool_scale, sgu_ln_g, sgu_ln_b, ssd_dt_bias, ssd_a_log, ssd_d,
                        ssd_norm_g, ssd_conv_b, sconv_w, ssd_conv_w):
    depth = norm_mix_g.shape[0]

    def rep(v):
        return jnp.repeat(v, GROUP_W // SSD_HEADS, axis=-1)

    def pad_to_row(v):
        return jnp.pad(v, [(0, 0)] * (v.ndim - 1) + [(0, D_MODEL - v.shape[-1])])

    rows = [
        norm_mix_g,
        jnp.concatenate([pool_b, pool_scale, sgu_ln_g, sgu_ln_b], axis=-1),
        jnp.concatenate([rep(ssd_dt_bias), rep(ssd_a_log), rep(ssd_d), ssd_norm_g], axis=-1),
        pad_to_row(ssd_conv_b),
        pad_to_row(sconv_w.reshape(depth, SCONV_WIDTH * GROUP_W)),
    ]
    table = jnp.concatenate([r[:, None, :] for r in rows] + [pad_to_row(ssd_conv_w)], axis=1)
    assert table.shape == (depth, N_VEC_ROWS, D_MODEL)
    return table.astype(F32)


def kernel(x, norm_mix_g, w_in, pool_w, pool_b, pool_scale, sgu_ln_g, sgu_ln_b, sgu_w, sgu_b, sconv_w, ssd_conv_w, ssd_conv_b, ssd_dt_bias, ssd_a_log, ssd_d, ssd_norm_g, w_out, norm_ffn_g, w_gate, w_up, w_down, final_norm_g):
    b, s, d = x.shape
    depth = w_in.shape[0]
    vecs = _pack_layer_vectors(norm_mix_g, pool_b, pool_scale, sgu_ln_g, sgu_ln_b, ssd_dt_bias, ssd_a_log,
                               ssd_d, ssd_norm_g, ssd_conv_b, sconv_w, ssd_conv_w)
    win = w_in.astype(BF16)
    wdt = jnp.repeat(w_in[:, :, C_DT:], GROUP_W // SSD_HEADS, axis=2).astype(BF16)
    poolw = jnp.einsum('lgcd,gh->lgchd', pool_w, jnp.eye(POOL_GROUPS, dtype=F32)).reshape(
        depth, GROUP_W, GROUP_W).astype(BF16)
    sgub = jnp.repeat(jnp.swapaxes(sgu_b, 1, 2), GROUP_W // SGU_HEADS, axis=2)
    wout = w_out.astype(BF16)
    fg = final_norm_g.reshape(1, d).astype(F32)
    mixer_params = (vecs, win, wdt, poolw, sgu_w, sgub, wout)

    h = x.reshape(b * s, d)
    for layer in range(depth):
        h = _mixer_call(h, s, layer, mixer_params)
        h = _ffn_call(h, layer, norm_ffn_g.reshape(depth, 1, d), w_gate, w_up, w_down, fg,
                      final_norm=(layer == depth - 1))
    return h.reshape(b, s, d)
```
